```python
import math
import jax
import jax.numpy as jnp
from jax import lax
import numpy as np

D_MODEL = 1024
BATCH = 8
SEQ = 8192
DEPTH = 2
DEC_BATCH = 4
DEC_SEQ = 4096
PAST_LEN = 128

D_MIX = 1024
D_PLE = 256
GRID_W = 64
ROPE_THETA = 10000.0
EPS = 1e-6
Q_BLOCK = 128
NEG_INF = -1e30

A_HEADS = 4
A_QK_DIM = 32
A_V_DIM = 64
B_HEADS = 4
B_KV_HEADS = 2
B_HEAD_DIM = 64
B_WINDOW = 128
C_HEADS = 4
C_HEAD_DIM = 64
C_MAX_ROWS = 8
C_WIN_COLS = 16
D_HEADS = 4
D_Q_RANK = 256
D_KV_RANK = 128
D_NOPE = 64
D_ROPE = 32
D_V = 64

IN_SIZES = (
    A_HEADS * 2 * A_QK_DIM, A_HEADS * 2 * A_QK_DIM, A_HEADS * A_V_DIM,
    B_HEADS * B_HEAD_DIM, B_KV_HEADS * B_HEAD_DIM, B_KV_HEADS * B_HEAD_DIM,
    C_HEADS * C_HEAD_DIM, C_HEADS * C_HEAD_DIM, C_HEADS * C_HEAD_DIM,
    D_Q_RANK, D_KV_RANK, D_ROPE,
    D_MIX,
)
D_IN = sum(IN_SIZES)

kernel_name = 'hybrid_parallel_head_encoder'


def _split_points():
    pts, acc = [], 0
    for n in IN_SIZES[:-1]:
        acc += n
        pts.append(acc)
    return pts


def rmsnorm(x, g):
    xf = x.astype(jnp.float32)
    y = xf * lax.rsqrt(jnp.mean(xf * xf, axis=-1, keepdims=True) + EPS)
    return (y * g.astype(jnp.float32)).astype(x.dtype)


def rope_tables(n, dim):
    inv = 1.0 / (ROPE_THETA ** (jnp.arange(0, dim, 2, dtype=jnp.float32) / dim))
    ang = jnp.arange(n, dtype=jnp.float32)[:, None] * inv[None, :]
    return jnp.cos(ang), jnp.sin(ang)


def apply_rope(x, cos, sin):
    half = x.shape[-1] // 2
    xf = x.astype(jnp.float32)
    x1, x2 = xf[..., :half], xf[..., half:]
    c, sn = cos[:, None, :], sin[:, None, :]
    return jnp.concatenate([x1 * c - x2 * sn, x2 * c + x1 * sn], axis=-1).astype(x.dtype)


def sweep_query_blocks(fn, *qs):
    b, s = qs[0].shape[:2]
    nb = s // Q_BLOCK
    blk = tuple(jnp.moveaxis(q.reshape((b, nb, Q_BLOCK) + q.shape[2:]), 1, 0) for q in qs)
    out = lax.map(lambda a: fn(*a), blk)
    out = jnp.moveaxis(out, 0, 1)
    return out.reshape((b, s) + out.shape[3:])


def diff_attention(q, k, v, lam_vecs, lam_init, subln_g, cos, sin):
    b, s = q.shape[:2]
    q = apply_rope(q.reshape(b, s, A_HEADS * 2, A_QK_DIM), cos, sin).reshape(b, s, A_HEADS, 2, A_QK_DIM)
    k = apply_rope(k.reshape(b, s, A_HEADS * 2, A_QK_DIM), cos, sin).reshape(b, s, A_HEADS, 2, A_QK_DIM)
    lv = lam_vecs.astype(jnp.float32)
    lam = jnp.exp(jnp.sum(lv[0] * lv[1])) - jnp.exp(jnp.sum(lv[2] * lv[3])) + lam_init
    scale = A_QK_DIM ** -0.5

    def block(qb):
        sc = jnp.einsum('bqhmd,bkhmd->bhmqk', qb, k).astype(jnp.float32) * scale
        pr = jax.nn.softmax(sc, axis=-1)
        pr = pr[:, :, 0] - lam * pr[:, :, 1]
        return jnp.einsum('bhqk,bkhd->bqhd', pr.astype(v.dtype), v)

    o = sweep_query_blocks(block, q)
    o = rmsnorm(o, subln_g) * (1.0 - lam_init)
    return o.reshape(b, s, A_HEADS * A_V_DIM)


def window_gqa(q, k, v, sink, cos, sin):
    b, s = q.shape[:2]
    g = B_HEADS // B_KV_HEADS
    w = B_WINDOW
    nb = s // w
    q = apply_rope(q, cos, sin)
    k = apply_rope(k, cos, sin)

    def band(t):
        tb = jnp.pad(t, ((0, 0), (w, w), (0, 0), (0, 0))).reshape(b, nb + 2, w, B_KV_HEADS, B_HEAD_DIM)
        return jnp.concatenate([tb[:, :-2], tb[:, 1:-1], tb[:, 2:]], axis=2)

    kb, vb = band(k), band(v)
    qb = q.reshape(b, nb, w, B_KV_HEADS, g, B_HEAD_DIM)
    sc = jnp.einsum('bnqkgd,bnckd->bnkgqc', qb, kb).astype(jnp.float32) * (B_HEAD_DIM ** -0.5)
    rel = jnp.arange(3 * w)[None, :] - w - jnp.arange(w)[:, None]
    key_pos = jnp.arange(nb)[:, None] * w - w + jnp.arange(3 * w)[None, :]
    valid = (jnp.abs(rel) <= w)[None] & ((key_pos >= 0) & (key_pos < s))[:, None, :]
    sc = jnp.where(valid[None, :, None, None], sc, NEG_INF)
    sink_col = jnp.broadcast_to(sink.astype(jnp.float32).reshape(1, 1, B_KV_HEADS, g, 1, 1), sc.shape[:-1] + (1,))
    pr = jax.nn.softmax(jnp.concatenate([sc, sink_col], axis=-1), axis=-1)[..., :-1]
    o = jnp.einsum('bnkgqc,bnckd->bnqkgd', pr.astype(v.dtype), vb)
    return o.reshape(b, s, B_HEADS * B_HEAD_DIM)


def neighbourhood_attention(q, k, v, rpb):
    b, s = q.shape[:2]
    rows = s // GRID_W
    kr = min(C_MAX_ROWS, rows)
    shp = (b, rows, GRID_W, C_HEADS, C_HEAD_DIM)
    qg, kg, vg = q.reshape(shp), k.reshape(shp), v.reshape(shp)
    r = jnp.arange(rows)
    row_idx = jnp.clip(r - kr // 2, 0, rows - kr)[:, None] + jnp.arange(kr)[None, :]
    kw, vw = kg[:, row_idx], vg[:, row_idx]
    sc = jnp.einsum('brqhd,brjchd->bhrqjc', qg, kw).astype(jnp.float32) * (C_HEAD_DIM ** -0.5)
    c = jnp.arange(GRID_W)
    col_start = jnp.clip(c - C_WIN_COLS // 2, 0, GRID_W - C_WIN_COLS)
    col_valid = (c[None, :] >= col_start[:, None]) & (c[None, :] < col_start[:, None] + C_WIN_COLS)
    dr = row_idx - r[:, None] + (C_MAX_ROWS - 1)
    dc = jnp.clip(c[None, :] - c[:, None], -(C_WIN_COLS - 1), C_WIN_COLS - 1) + (C_WIN_COLS - 1)
    bias = rpb.astype(jnp.float32)[:, dr[:, None, :, None], dc[None, :, None, :]]
    sc = jnp.where(col_valid[:, None, :], sc + bias[None], NEG_INF)
    pr = jax.nn.softmax(sc, axis=(-2, -1))
    o = jnp.einsum('bhrqjc,brjchd->brqhd', pr.astype(v.dtype), vw)
    return o.reshape(b, s, C_HEADS * C_HEAD_DIM)


def mla(c_q, c_kv, k_rope, q_norm_g, kv_norm_g, w_uq, w_ukv, cos, sin):
    b, s = c_q.shape[:2]
    q = (rmsnorm(c_q, q_norm_g) @ w_uq).reshape(b, s, D_HEADS, D_NOPE + D_ROPE)
    kv = (rmsnorm(c_kv, kv_norm_g) @ w_ukv).reshape(b, s, D_HEADS, D_NOPE + D_V)
    q_nope, q_pe = q[..., :D_NOPE], apply_rope(q[..., D_NOPE:], cos, sin)
    k_nope, v = kv[..., :D_NOPE], kv[..., D_NOPE:]
    k_pe = apply_rope(k_rope.reshape(b, s, 1, D_ROPE), cos, sin)[:, :, 0]
    scale = (D_NOPE + D_ROPE) ** -0.5

    def block(qn, qp):
        sc = (jnp.einsum('bqhd,bkhd->bhqk', qn, k_nope).astype(jnp.float32)
              + jnp.einsum('bqhd,bkd->bhqk', qp, k_pe).astype(jnp.float32)) * scale
        pr = jax.nn.softmax(sc, axis=-1)
        return jnp.einsum('bhqk,bkhd->bqhd', pr.astype(v.dtype), v)

    o = sweep_query_blocks(block, q_nope, q_pe)
    return o.reshape(b, s, D_HEADS * D_V)


def trunk(x, p, norm_g, w_in, a_lambda, a_subln_g, b_sink, c_rpb, d_q_norm_g, d_kv_norm_g,
          d_w_uq, d_w_ukv, w_out, ple_norm_g, w_ple_gate, w_ple_proj, final_norm_g):
    b, s, _ = x.shape
    cos32, sin32 = rope_tables(s, A_QK_DIM)
    cos64, sin64 = rope_tables(s, B_HEAD_DIM)
    cosd, sind = rope_tables(s, D_ROPE)
    pts = _split_points()
    h = x
    for i in range(DEPTH):
        hn = rmsnorm(h, norm_g[i])
        z = hn @ w_in[i]
        (aq, ak, av, bq, bk, bv, cq, ck, cv, dcq, dckv, dkr, gate) = jnp.split(z, pts, axis=-1)
        lam_init = 0.8 - 0.6 * math.exp(-0.3 * i)
        o_a = diff_attention(aq.reshape(b, s, A_HEADS, 2, A_QK_DIM), ak.reshape(b, s, A_HEADS, 2, A_QK_DIM),
                             av.reshape(b, s, A_HEADS, A_V_DIM), a_lambda[i], lam_init, a_subln_g[i],
                             cos32, sin32)
        o_b = window_gqa(bq.reshape(b, s, B_HEADS, B_HEAD_DIM), bk.reshape(b, s, B_KV_HEADS, B_HEAD_DIM),
                         bv.reshape(b, s, B_KV_HEADS, B_HEAD_DIM), b_sink[i], cos64, sin64)
        o_c = neighbourhood_attention(cq.reshape(b, s, C_HEADS, C_HEAD_DIM), ck.reshape(b, s, C_HEADS, C_HEAD_DIM),
                                      cv.reshape(b, s, C_HEADS, C_HEAD_DIM), c_rpb[i])
        o_d = mla(dcq, dckv, dkr, d_q_norm_g[i], d_kv_norm_g[i], d_w_uq[i], d_w_ukv[i], cosd, sind)
        mix = jnp.concatenate([o_a, o_b, o_c, o_d], axis=-1) * jax.nn.silu(gate)
        h = h + mix @ w_out[i]
        ple_gate = jax.nn.sigmoid(rmsnorm(h, ple_norm_g[i]) @ w_ple_gate[i])
        h = h + ple_gate * (p[i] @ w_ple_proj[i])
    return rmsnorm(h, final_norm_g)


def setup_inputs(seed: int = 0) -> dict:
    key = jax.random.key(seed)
    ks = jax.random.split(key, 19)
    f32 = jnp.float32

    def nrm(k, shape, scale=1.0):
        return jax.random.normal(k, shape, f32) * scale

    def gain(k, shape):
        return 1.0 + 0.05 * jax.random.normal(k, shape, f32)

    return {
        'x_prompt': nrm(ks[0], (BATCH, SEQ, D_MODEL)),
        'x_sample': nrm(ks[1], (DEC_BATCH, DEC_SEQ, D_MODEL)),
        'p_prompt': nrm(ks[2], (DEPTH, BATCH, SEQ, D_PLE)),
        'p_sample': nrm(ks[3], (DEPTH, DEC_BATCH, DEC_SEQ, D_PLE)),
        'norm_g': gain(ks[4], (DEPTH, D_MODEL)),
        'w_in': nrm(ks[5], (DEPTH, D_MODEL, D_IN), D_MODEL ** -0.5),
        'a_lambda': nrm(ks[6], (DEPTH, 4, A_QK_DIM), 0.1),
        'a_subln_g': gain(ks[7], (DEPTH, A_V_DIM)),
        'b_sink': nrm(ks[8], (DEPTH, B_HEADS), 0.5),
        'c_rpb': nrm(ks[9], (DEPTH, C_HEADS, 2 * C_MAX_ROWS - 1, 2 * C_WIN_COLS - 1), 0.1),
        'd_q_norm_g': gain(ks[10], (DEPTH, D_Q_RANK)),
        'd_kv_norm_g': gain(ks[11], (DEPTH, D_KV_RANK)),
        'd_w_uq': nrm(ks[12], (DEPTH, D_Q_RANK, D_HEADS * (D_NOPE + D_ROPE)), D_Q_RANK ** -0.5),
        'd_w_ukv': nrm(ks[13], (DEPTH, D_KV_RANK, D_HEADS * (D_NOPE + D_V)), D_KV_RANK ** -0.5),
        'w_out': nrm(ks[14], (DEPTH, D_MIX, D_MODEL), D_MIX ** -0.5),
        'ple_norm_g': gain(ks[15], (DEPTH, D_MODEL)),
        'w_ple_gate': nrm(ks[16], (DEPTH, D_MODEL, D_MODEL), D_MODEL ** -0.5),
        'w_ple_proj': nrm(ks[17], (DEPTH, D_PLE, D_MODEL), D_PLE ** -0.5),
        'final_norm_g': gain(ks[18], (D_MODEL,)),
    }


def reference(x_prompt, x_sample, p_prompt, p_sample, norm_g, w_in, a_lambda, a_subln_g, b_sink, c_rpb,
              d_q_norm_g, d_kv_norm_g, d_w_uq, d_w_ukv, w_out, ple_norm_g, w_ple_gate, w_ple_proj,
              final_norm_g):
    y_prompt = trunk(x_prompt, p_prompt, norm_g, w_in, a_lambda, a_subln_g, b_sink, c_rpb, d_q_norm_g,
                     d_kv_norm_g, d_w_uq, d_w_ukv, w_out, ple_norm_g, w_ple_gate, w_ple_proj, final_norm_g)
    y_sample = trunk(x_sample, p_sample, norm_g, w_in, a_lambda, a_subln_g, b_sink, c_rpb, d_q_norm_g,
                     d_kv_norm_g, d_w_uq, d_w_ukv, w_out, ple_norm_g, w_ple_gate, w_ple_proj, final_norm_g)
    return (y_prompt, y_sample)
```

```python
import functools
import math

import jax
import jax.numpy as jnp
from jax import lax
from jax.experimental import pallas as pl
from jax.experimental.pallas import tpu as pltpu

F32 = jnp.float32
BF16 = jnp.bfloat16

D_MODEL = 1024
D_MIX = 1024
D_PLE = 256
GRID_W = 64
ROPE_THETA = 10000.0
EPS = 1e-6
NEG_INF = -1e30

A_HEADS, A_QK_DIM, A_V_DIM = 4, 32, 64
B_HEADS, B_KV_HEADS, B_HEAD_DIM, B_WINDOW = 4, 2, 64, 128
C_HEADS, C_HEAD_DIM, C_MAX_ROWS, C_WIN_COLS = 4, 64, 8, 16
D_HEADS, D_Q_RANK, D_KV_RANK, D_NOPE, D_ROPE, D_V = 4, 256, 128, 64, 32, 64

IN_SIZES = (256, 256, 256, 256, 128, 128, 256, 256, 256, D_Q_RANK, D_KV_RANK, D_ROPE, D_MIX)

V_ROWS = 80
VMEM_LIMIT = 56 * 1024 * 1024

TM = 512
TQ = 256
TK = 512
C_QROWS = 4
C_KROWS = 12


def _nt(a, b):
    return lax.dot_general(a, b, (((1,), (1,)), ((), ())), preferred_element_type=F32)


def _tn(a, b):
    return lax.dot_general(a, b, (((0,), (0,)), ((), ())), preferred_element_type=F32)


def _dot(a, b):
    return jnp.dot(a, b, preferred_element_type=F32)


T_AQ, T_AQR, T_BQ, T_BQR, T_CQ, T_DCQ, T_DCKV, T_AV, T_BV, T_CV, T_GATE, T_END = (
    0, 256, 512, 768, 1024, 1280, 1536, 1664, 1920, 2048, 2304, 3328)
S_AK, S_AKR, S_BK, S_BKR, S_CK, S_DCKV, S_KPE, S_KPER, S_END = (
    0, 256, 512, 640, 768, 1024, 1152, 1280, 1408)


def _proj_kernel(h_ref, g_ref, wt_ref, ws_ref, ta_ref, tb_ref, td_ref, sa_ref, sb_ref, sd_ref,
                 qg_ref, kvgc_ref, kvgr_ref, wqn_ref, wqp_ref, wqpr_ref, wkn_ref, wvt_ref,
                 qa_ref, ka_ref, va_ref, qb_ref, kb_ref, vb_ref, qc_ref, kc_ref, vc_ref,
                 qd_ref, kd_ref, vd_ref, sg_ref, *, d_scale):
    x = h_ref[0]
    tm = x.shape[0]
    ms = jnp.mean(x * x, axis=-1, keepdims=True)
    hn = ((x * lax.rsqrt(ms + EPS)) * g_ref[...]).astype(BF16)

    def zt(r0, r1):
        return _nt(wt_ref[r0:r1, :], hn)

    ones_rows = (lax.broadcasted_iota(jnp.int32, (V_ROWS - 64, tm), 0) == 0).astype(BF16)

    def store_values(ref, v, heads):
        for h in range(heads):
            ref[0, h, 0:64, :] = v[64 * h:64 * h + 64].astype(BF16)
            ref[0, h, 64:V_ROWS, :] = ones_rows

    qa = zt(T_AQ, T_AQR) * ta_ref[0] + zt(T_AQR, T_BQ) * ta_ref[1]
    for g in range(2 * A_HEADS):
        qa_ref[0, g] = jnp.zeros((256, tm), BF16)
        qa_ref[0, g, 32 * g:32 * g + 32, :] = qa[32 * g:32 * g + 32].astype(BF16)
    store_values(va_ref, zt(T_AV, T_BV), A_HEADS)

    qb = zt(T_BQ, T_BQR) * tb_ref[0] + zt(T_BQR, T_CQ) * tb_ref[1]
    for h in range(B_HEADS):
        r = 64 * (h // 2)
        qb_ref[0, h] = jnp.zeros((128, tm), BF16)
        qb_ref[0, h, r:r + 64, :] = qb[64 * h:64 * h + 64].astype(BF16)
    store_values(vb_ref, zt(T_BV, T_CV), B_KV_HEADS)

    qc = zt(T_CQ, T_DCQ) * (C_HEAD_DIM ** -0.5)
    for h in range(C_HEADS):
        qc_ref[0, h] = jnp.zeros((256, tm), BF16)
        qc_ref[0, h, 64 * h:64 * h + 64, :] = qc[64 * h:64 * h + 64].astype(BF16)
    store_values(vc_ref, zt(T_CV, T_GATE), C_HEADS)

    cq = zt(T_DCQ, T_DCKV)
    cqn = ((cq * lax.rsqrt(jnp.mean(cq * cq, axis=0, keepdims=True) + EPS)) * qg_ref[...]).astype(BF16)
    for h in range(D_HEADS):
        qn = _dot(wqn_ref[h], cqn) * d_scale
        qpe = _dot(wqp_ref[h], cqn) * td_ref[0] + _dot(wqpr_ref[h], cqn) * td_ref[1]
        qd_ref[0, h, 0:128, :] = qn.astype(BF16)
        qd_ref[0, h, 128:160, :] = qpe.astype(BF16)
        qd_ref[0, h, 160:256, :] = jnp.zeros((96, tm), BF16)
    ckt = zt(T_DCKV, T_AV)
    cktn = ((ckt * lax.rsqrt(jnp.mean(ckt * ckt, axis=0, keepdims=True) + EPS)) * kvgc_ref[...]).astype(BF16)
    store_values(vd_ref, _dot(wvt_ref[...], cktn), D_HEADS)

    gt = zt(T_GATE, T_END)
    sg_ref[0] = gt * jax.nn.sigmoid(gt)

    zs = _dot(hn, ws_ref[...])
    ka_ref[0, 0] = (zs[:, S_AK:S_AKR] * sa_ref[0] + zs[:, S_AKR:S_BK] * sa_ref[1]).astype(BF16)
    kb_ref[0, 0] = (zs[:, S_BK:S_BKR] * sb_ref[0] + zs[:, S_BKR:S_CK] * sb_ref[1]).astype(BF16)
    kc_ref[0, 0] = zs[:, S_CK:S_DCKV].astype(BF16)
    ck = zs[:, S_DCKV:S_KPE]
    ckn = ((ck * lax.rsqrt(jnp.mean(ck * ck, axis=-1, keepdims=True) + EPS)) * kvgr_ref[...]).astype(BF16)
    kn = _dot(ckn, wkn_ref[...])
    kpe = (zs[:, S_KPE:S_KPER] * sd_ref[0] + zs[:, S_KPER:S_END] * sd_ref[1]).astype(BF16)
    for p in range(2):
        kd_ref[0, p, :, 0:128] = kn[:, 128 * p:128 * p + 128].astype(BF16)
        kd_ref[0, p, :, 128:256] = kpe


def _rope_cs(s, dim):
    inv = 1.0 / (ROPE_THETA ** (jnp.arange(0, dim, 2, dtype=F32) / dim))
    ang = jnp.arange(s, dtype=F32)[:, None] * inv[None, :]
    cos, sin = jnp.cos(ang), jnp.sin(ang)
    return jnp.concatenate([cos, cos], -1), jnp.concatenate([-sin, sin], -1)


def _rope_tables(s):
    c32, s32 = _rope_cs(s, A_QK_DIM)
    c64, s64 = _rope_cs(s, B_HEAD_DIM)
    a_scale, b_scale, d_scale = A_QK_DIM ** -0.5, B_HEAD_DIM ** -0.5, (D_NOPE + D_ROPE) ** -0.5
    ta = jnp.stack([jnp.tile(c32, (1, 8)).T, jnp.tile(s32, (1, 8)).T]) * a_scale
    tb = jnp.stack([jnp.tile(c64, (1, 4)).T, jnp.tile(s64, (1, 4)).T]) * b_scale
    td = jnp.stack([c32.T, s32.T]) * d_scale
    sa = jnp.stack([jnp.tile(c32, (1, 8)), jnp.tile(s32, (1, 8))])
    sb = jnp.stack([jnp.tile(c64, (1, 2)), jnp.tile(s64, (1, 2))])
    pad = jnp.zeros((s, 128 - D_ROPE), F32)
    sd = jnp.stack([jnp.concatenate([c32, pad], -1), jnp.concatenate([s32, pad], -1)])
    return ta, tb, td, sa, sb, sd


def _rot_cols(w, d):
    k, n = w.shape
    return jnp.roll(w.reshape(k, n // d, d), d // 2, axis=-1).reshape(k, n)


def _prep_layer(w_in, d_w_uq, d_w_ukv):
    pts = [0]
    for n in IN_SIZES:
        pts.append(pts[-1] + n)
    (aq, ak, av, bq, bk, bv, cq, ck, cv, dcq, dckv, dkr, gate) = [w_in[:, pts[i]:pts[i + 1]] for i in range(13)]
    wt = jnp.concatenate([aq, _rot_cols(aq, A_QK_DIM), bq, _rot_cols(bq, B_HEAD_DIM), cq, dcq, dckv,
                          av, bv, cv, gate], axis=1).T.astype(BF16)
    zpad = jnp.zeros((D_MODEL, 128 - D_ROPE), F32)
    ws = jnp.concatenate([ak, _rot_cols(ak, A_QK_DIM), bk, _rot_cols(bk, B_HEAD_DIM), ck, dckv,
                          dkr, zpad, _rot_cols(dkr, D_ROPE), zpad], axis=1).astype(BF16)
    uq = d_w_uq.reshape(D_Q_RANK, D_HEADS, D_NOPE + D_ROPE)
    uq_n, uq_p = uq[:, :, :D_NOPE], uq[:, :, D_NOPE:]
    wqn = jnp.zeros((D_HEADS, 128, D_Q_RANK), F32)
    for h in range(D_HEADS):
        r = 64 * (h % 2)
        wqn = wqn.at[h, r:r + 64].set(uq_n[:, h].T)
    wqp = jnp.transpose(uq_p, (1, 2, 0))
    wqpr = jnp.transpose(jnp.roll(uq_p, D_ROPE // 2, axis=-1), (1, 2, 0))
    ukv = d_w_ukv.reshape(D_KV_RANK, D_HEADS, D_NOPE + D_V)
    wkn = ukv[:, :, :D_NOPE].reshape(D_KV_RANK, D_HEADS * D_NOPE)
    wvt = ukv[:, :, D_NOPE:].reshape(D_KV_RANK, D_HEADS * D_V).T
    return (wt, ws, wqn.astype(BF16), wqp.astype(BF16), wqpr.astype(BF16), wkn.astype(BF16), wvt.astype(BF16))


def _const_spec(shape):
    nd = len(shape)
    return pl.BlockSpec(shape, lambda *_: (0,) * nd)


def _proj(h, norm_g, prep, tabs, q_norm_g, kv_norm_g):
    b, s, _ = h.shape
    tm = min(TM, s)
    wt, ws, wqn, wqp, wqpr, wkn, wvt = prep
    ta, tb, td, sa, sb, sd = tabs
    d_scale = (D_NOPE + D_ROPE) ** -0.5

    def tok(shape_tail):
        return pl.BlockSpec((1, 1, tm) + shape_tail, lambda i, j: (j, 0, i, 0))

    def chan(groups, rows):
        return pl.BlockSpec((1, groups, rows, tm), lambda i, j: (j, 0, 0, i))

    in_specs = [
        pl.BlockSpec((1, tm, D_MODEL), lambda i, j: (j, i, 0)),
        _const_spec((1, D_MODEL)), _const_spec(wt.shape), _const_spec(ws.shape),
        pl.BlockSpec((2, 256, tm), lambda i, j: (0, 0, i)),
        pl.BlockSpec((2, 256, tm), lambda i, j: (0, 0, i)),
        pl.BlockSpec((2, D_ROPE, tm), lambda i, j: (0, 0, i)),
        pl.BlockSpec((2, tm, 256), lambda i, j: (0, i, 0)),
        pl.BlockSpec((2, tm, 128), lambda i, j: (0, i, 0)),
        pl.BlockSpec((2, tm, 128), lambda i, j: (0, i, 0)),
        _const_spec((D_Q_RANK, 1)), _const_spec((D_KV_RANK, 1)), _const_spec((1, D_KV_RANK)),
        _const_spec(wqn.shape), _const_spec(wqp.shape), _const_spec(wqpr.shape),
        _const_spec(wkn.shape), _const_spec(wvt.shape),
    ]
    out_shape = [
        jax.ShapeDtypeStruct((b, 8, 256, s), BF16),
        jax.ShapeDtypeStruct((b, 1, s, 256), BF16),
        jax.ShapeDtypeStruct((b, A_HEADS, V_ROWS, s), BF16),
        jax.ShapeDtypeStruct((b, B_HEADS, 128, s), BF16),
        jax.ShapeDtypeStruct((b, 1, s, 128), BF16),
        jax.ShapeDtypeStruct((b, B_KV_HEADS, V_ROWS, s), BF16),
        jax.ShapeDtypeStruct((b, C_HEADS, 256, s), BF16),
        jax.ShapeDtypeStruct((b, 1, s, 256), BF16),
        jax.ShapeDtypeStruct((b, C_HEADS, V_ROWS, s), BF16),
        jax.ShapeDtypeStruct((b, D_HEADS, 256, s), BF16),
        jax.ShapeDtypeStruct((b, 2, s, 256), BF16),
        jax.ShapeDtypeStruct((b, D_HEADS, V_ROWS, s), BF16),
        jax.ShapeDtypeStruct((b, D_MIX, s), F32),
    ]
    out_specs = [
        chan(8, 256), tok((256,)), chan(A_HEADS, V_ROWS),
        chan(B_HEADS, 128), tok((128,)), chan(B_KV_HEADS, V_ROWS),
        chan(C_HEADS, 256), tok((256,)), chan(C_HEADS, V_ROWS),
        chan(D_HEADS, 256), pl.BlockSpec((1, 2, tm, 256), lambda i, j: (j, 0, i, 0)), chan(D_HEADS, V_ROWS),
        pl.BlockSpec((1, D_MIX, tm), lambda i, j: (j, 0, i)),
    ]
    return pl.pallas_call(
        functools.partial(_proj_kernel, d_scale=d_scale),
        grid=(s // tm, b),
        in_specs=in_specs, out_specs=out_specs, out_shape=out_shape,
        compiler_params=pltpu.CompilerParams(
            dimension_semantics=("arbitrary", "arbitrary"), vmem_limit_bytes=VMEM_LIMIT),
        name="proj",
    )(h, norm_g.reshape(1, D_MODEL), wt, ws, ta, tb, td, sa, sb, sd,
      q_norm_g.reshape(D_Q_RANK, 1), kv_norm_g.reshape(D_KV_RANK, 1), kv_norm_g.reshape(1, D_KV_RANK),
      wqn, wqp, wqpr, wkn, wvt)


def _flash_kernel(k_ref, q_ref, v_ref, o_ref, *, tk, nk):
    q = q_ref[0, 0]
    tq = q.shape[1]

    def body(j, carry):
        m, acc = carry
        off = pl.multiple_of(j * tk, tk)
        k = k_ref[0, 0, pl.ds(off, tk), :]
        s = _dot(k, q)
        m_new = jnp.maximum(m, jnp.max(s, axis=0, keepdims=True))
        alpha = jnp.exp(m - m_new)
        p = jnp.exp(s - m_new).astype(BF16)
        v = v_ref[0, 0, :, pl.ds(off, tk)]
        return m_new, acc * alpha + _dot(v, p)

    m0 = jnp.full((1, tq), NEG_INF, F32)
    acc0 = jnp.zeros((V_ROWS, tq), F32)
    _, acc = lax.fori_loop(0, nk, body, (m0, acc0))
    o_ref[0, 0] = acc[0:64] / acc[64:65]


def _flash(k, qt, vt, k_group, v_group):
    b, g, dk, s = qt.shape
    tq, tk = min(TQ, s), min(TK, s)
    return pl.pallas_call(
        functools.partial(_flash_kernel, tk=tk, nk=s // tk),
        grid=(b, g, s // tq),
        in_specs=[
            pl.BlockSpec((1, 1, s, dk), lambda bi, gi, i: (bi, k_group(gi), 0, 0)),
            pl.BlockSpec((1, 1, dk, tq), lambda bi, gi, i: (bi, gi, 0, i)),
            pl.BlockSpec((1, 1, V_ROWS, s), lambda bi, gi, i: (bi, v_group(gi), 0, 0)),
        ],
        out_specs=pl.BlockSpec((1, 1, 64, tq), lambda bi, gi, i: (bi, gi, 0, i)),
        out_shape=jax.ShapeDtypeStruct((b, g, 64, s), F32),
        compiler_params=pltpu.CompilerParams(
            dimension_semantics=("arbitrary", "arbitrary", "arbitrary"), vmem_limit_bytes=VMEM_LIMIT),
        name="flash",
    )(k, qt, vt)


def _window_kernel(sink_ref, k_ref, q_ref, v_ref, o_ref, *, s_len, win):
    h, i = pl.program_id(1), pl.program_id(2)
    q = q_ref[0, 0]
    tq = q.shape[1]
    q0 = i * tq
    start = pl.multiple_of(jnp.clip(q0 - B_WINDOW, 0, s_len - win), 128)
    k = k_ref[0, 0, pl.ds(start, win), :]
    s = _dot(k, q)
    kpos = start + lax.broadcasted_iota(jnp.int32, (win, tq), 0)
    qpos = q0 + lax.broadcasted_iota(jnp.int32, (win, tq), 1)
    s = jnp.where(jnp.abs(kpos - qpos) <= B_WINDOW, s, NEG_INF)
    sink = sink_ref[h]
    m = jnp.maximum(jnp.max(s, axis=0, keepdims=True), sink)
    p = jnp.exp(s - m).astype(BF16)
    acc = _dot(v_ref[0, 0, :, pl.ds(start, win)], p)
    o_ref[0, 0] = acc[0:64] / (acc[64:65] + jnp.exp(sink - m))


def _window(k, qt, vt, sink):
    b, g, dk, s = qt.shape
    tq = min(TQ, s)
    win = tq + 2 * B_WINDOW
    return pl.pallas_call(
        functools.partial(_window_kernel, s_len=s, win=win),
        grid=(b, g, s // tq),
        in_specs=[
            pl.BlockSpec(memory_space=pltpu.SMEM),
            pl.BlockSpec((1, 1, s, dk), lambda bi, gi, i: (bi, 0, 0, 0)),
            pl.BlockSpec((1, 1, dk, tq), lambda bi, gi, i: (bi, gi, 0, i)),
            pl.BlockSpec((1, 1, V_ROWS, s), lambda bi, gi, i: (bi, gi // (B_HEADS // B_KV_HEADS), 0, 0)),
        ],
        out_specs=pl.BlockSpec((1, 1, 64, tq), lambda bi, gi, i: (bi, gi, 0, i)),
        out_shape=jax.ShapeDtypeStruct((b, g, 64, s), F32),
        compiler_params=pltpu.CompilerParams(
            dimension_semantics=("arbitrary", "arbitrary", "arbitrary"), vmem_limit_bytes=VMEM_LIMIT),
        name="window",
    )(sink.astype(F32), k, qt, vt)


def _nbr_kernel(bias_ref, k_ref, q_ref, v_ref, o_ref, *, s_len, win):
    i = pl.program_id(2)
    q = q_ref[0, 0]
    tq = q.shape[1]
    start = pl.multiple_of(jnp.clip(i * tq - (C_MAX_ROWS // 2) * GRID_W, 0, s_len - win), 256)
    k = k_ref[0, 0, pl.ds(start, win), :]
    s = _dot(k, q) + bias_ref[0, 0]
    m = jnp.max(s, axis=0, keepdims=True)
    p = jnp.exp(s - m).astype(BF16)
    acc = _dot(v_ref[0, 0, :, pl.ds(start, win)], p)
    o_ref[0, 0] = acc[0:64] / acc[64:65]


def _nbr_bias(rpb, rows):
    kk = jnp.arange(C_KROWS * GRID_W)
    qq = jnp.arange(C_QROWS * GRID_W)
    jk, ck = kk // GRID_W, kk % GRID_W
    jq, cq = qq // GRID_W, qq % GRID_W
    cs = jnp.clip(cq - C_WIN_COLS // 2, 0, GRID_W - C_WIN_COLS)
    col_ok = (ck[:, None] >= cs[None, :]) & (ck[:, None] < cs[None, :] + C_WIN_COLS)
    dc = jnp.clip(ck[:, None] - cq[None, :], -(C_WIN_COLS - 1), C_WIN_COLS - 1) + (C_WIN_COLS - 1)
    kr = min(C_MAX_ROWS, rows)
    out = []
    for r0 in (0, C_QROWS, rows - C_QROWS):
        rk0 = min(max(r0 - C_MAX_ROWS // 2, 0), rows - C_KROWS)
        rq = r0 + jq
        rk = rk0 + jk
        rs = jnp.clip(rq - kr // 2, 0, rows - kr)
        row_ok = (rk[:, None] >= rs[None, :]) & (rk[:, None] < rs[None, :] + kr)
        dr = jnp.clip(rk[:, None] - rq[None, :] + (C_MAX_ROWS - 1), 0, 2 * C_MAX_ROWS - 2)
        bias = rpb.astype(F32)[:, dr, dc]
        out.append(jnp.where((row_ok & col_ok)[None], bias, NEG_INF))
    return jnp.stack(out)


def _nbr(k, qt, vt, rpb):
    b, g, dk, s = qt.shape
    tq = C_QROWS * GRID_W
    win = C_KROWS * GRID_W
    nq = s // tq
    bias = _nbr_bias(rpb, s // GRID_W)

    def variant(i):
        return jnp.where(i == 0, 0, jnp.where(i == nq - 1, 2, 1))

    return pl.pallas_call(
        functools.partial(_nbr_kernel, s_len=s, win=win),
        grid=(b, g, nq),
        in_specs=[
            pl.BlockSpec((1, 1, win, tq), lambda bi, gi, i: (variant(i), gi, 0, 0)),
            pl.BlockSpec((1, 1, s, dk), lambda bi, gi, i: (bi, 0, 0, 0)),
            pl.BlockSpec((1, 1, dk, tq), lambda bi, gi, i: (bi, gi, 0, i)),
            pl.BlockSpec((1, 1, V_ROWS, s), lambda bi, gi, i: (bi, gi, 0, 0)),
        ],
        out_specs=pl.BlockSpec((1, 1, 64, tq), lambda bi, gi, i: (bi, gi, 0, i)),
        out_shape=jax.ShapeDtypeStruct((b, g, 64, s), F32),
        compiler_params=pltpu.CompilerParams(
            dimension_semantics=("arbitrary", "arbitrary", "arbitrary"), vmem_limit_bytes=VMEM_LIMIT),
        name="nbr",
    )(bias, k, qt, vt)


def _out_kernel(oa_ref, ob_ref, oc_ref, od_ref, sg_ref, h_ref, p_ref, lam_ref, sub_ref,
                wo_ref, pg_ref, wg_ref, wp_ref, fg_ref, y_ref, *, lam_init, final):
    lv = lam_ref[...]
    lam = (jnp.exp(jnp.sum(lv[0:1] * lv[1:2], axis=-1, keepdims=True))
           - jnp.exp(jnp.sum(lv[2:3] * lv[3:4], axis=-1, keepdims=True)) + lam_init)
    parts = []
    for h in range(A_HEADS):
        o = oa_ref[0, 2 * h] - lam * oa_ref[0, 2 * h + 1]
        on = (o * lax.rsqrt(jnp.mean(o * o, axis=0, keepdims=True) + EPS)) * sub_ref[...]
        parts.append(on * (1.0 - lam_init))
    for ref in (ob_ref, oc_ref, od_ref):
        for h in range(4):
            parts.append(ref[0, h])
    mix = (jnp.concatenate(parts, axis=0) * sg_ref[0]).astype(BF16)
    h1 = h_ref[0] + _tn(mix, wo_ref[...])
    hn = ((h1 * lax.rsqrt(jnp.mean(h1 * h1, axis=-1, keepdims=True) + EPS)) * pg_ref[...]).astype(BF16)
    gate = jax.nn.sigmoid(_dot(hn, wg_ref[...]))
    h2 = h1 + gate * _dot(p_ref[0].astype(BF16), wp_ref[...])
    if final:
        h2 = (h2 * lax.rsqrt(jnp.mean(h2 * h2, axis=-1, keepdims=True) + EPS)) * fg_ref[...]
    y_ref[0] = h2


def _out(oa, ob, oc, od, sg, h, p, a_lambda, subln_g, w_out, ple_norm_g, w_ple_gate, w_ple_proj,
         final_norm_g, lam_init, final):
    b, s, _ = h.shape
    tm = min(TM, s)

    def heads(n):
        return pl.BlockSpec((1, n, 64, tm), lambda bi, i: (bi, 0, 0, i))

    return pl.pallas_call(
        functools.partial(_out_kernel, lam_init=lam_init, final=final),
        grid=(b, s // tm),
        in_specs=[
            heads(8), heads(4), heads(4), heads(4),
            pl.BlockSpec((1, D_MIX, tm), lambda bi, i: (bi, 0, i)),
            pl.BlockSpec((1, tm, D_MODEL), lambda bi, i: (bi, i, 0)),
            pl.BlockSpec((1, tm, D_PLE), lambda bi, i: (bi, i, 0)),
            _const_spec((4, A_QK_DIM)), _const_spec((A_V_DIM, 1)),
            _const_spec((D_MIX, D_MODEL)), _const_spec((1, D_MODEL)),
            _const_spec((D_MODEL, D_MODEL)), _const_spec((D_PLE, D_MODEL)), _const_spec((1, D_MODEL)),
        ],
        out_specs=pl.BlockSpec((1, tm, D_MODEL), lambda bi, i: (bi, i, 0)),
        out_shape=jax.ShapeDtypeStruct((b, s, D_MODEL), F32),
        compiler_params=pltpu.CompilerParams(
            dimension_semantics=("arbitrary", "arbitrary"), vmem_limit_bytes=VMEM_LIMIT),
        name="out",
    )(oa, ob, oc, od, sg, h, p, a_lambda.astype(F32), subln_g.reshape(A_V_DIM, 1).astype(F32),
      w_out.astype(BF16), ple_norm_g.reshape(1, D_MODEL), w_ple_gate.astype(BF16), w_ple_proj.astype(BF16),
      final_norm_g.reshape(1, D_MODEL))


def _trunk(x, p, norm_g, w_in, a_lambda, a_subln_g, b_sink, c_rpb, d_q_norm_g, d_kv_norm_g,
           d_w_uq, d_w_ukv, w_out, ple_norm_g, w_ple_gate, w_ple_proj, final_norm_g):
    depth = w_in.shape[0]
    s = x.shape[1]
    tabs = _rope_tables(s)
    h = x
    for i in range(depth):
        prep = _prep_layer(w_in[i], d_w_uq[i], d_w_ukv[i])
        (qa, ka, va, qb, kb, vb, qc, kc, vc, qd, kd, vd, sg) = _proj(
            h, norm_g[i], prep, tabs, d_q_norm_g[i], d_kv_norm_g[i])
        oa = _flash(ka, qa, va, lambda g: 0, lambda g: g // 2)
        od = _flash(kd, qd, vd, lambda g: g // 2, lambda g: g)
        ob = _window(kb, qb, vb, b_sink[i])
        oc = _nbr(kc, qc, vc, c_rpb[i])
        lam_init = 0.8 - 0.6 * math.exp(-0.3 * i)
        h = _out(oa, ob, oc, od, sg, h, p[i], a_lambda[i], a_subln_g[i], w_out[i], ple_norm_g[i],
                 w_ple_gate[i], w_ple_proj[i], final_norm_g, lam_init, i == depth - 1)
    return h


def kernel(x_prompt, x_sample, p_prompt, p_sample, norm_g, w_in, a_lambda, a_subln_g, b_sink, c_rpb,
           d_q_norm_g, d_kv_norm_g, d_w_uq, d_w_ukv, w_out, ple_norm_g, w_ple_gate, w_ple_proj,
           final_norm_g):
    weights = (norm_g, w_in, a_lambda, a_subln_g, b_sink, c_rpb, d_q_norm_g, d_kv_norm_g, d_w_uq, d_w_ukv,
               w_out, ple_norm_g, w_ple_gate, w_ple_proj, final_norm_g)
    return (_trunk(x_prompt, p_prompt, *weights), _trunk(x_sample, p_sample, *weights))
```

```python
import functools
import math

import jax
import jax.numpy as jnp
import numpy as np
from jax import lax
from jax.experimental import pallas as pl
from jax.experimental.pallas import tpu as pltpu

F32 = jnp.float32
BF16 = jnp.bfloat16

D_MODEL = 1024
D_MIX = 1024
D_PLE = 256
GRID_W = 64
ROPE_THETA = 10000.0
EPS = 1e-6
NEG_INF = -1e30

A_HEADS, A_QK_DIM, A_V_DIM = 4, 32, 64
B_HEADS, B_KV_HEADS, B_HEAD_DIM, B_WINDOW = 4, 2, 64, 128
C_HEADS, C_HEAD_DIM, C_MAX_ROWS, C_WIN_COLS = 4, 64, 8, 16
D_HEADS, D_Q_RANK, D_KV_RANK, D_NOPE, D_ROPE, D_V = 4, 256, 128, 64, 32, 64

IN_SIZES = (256, 256, 256, 256, 128, 128, 256, 256, 256, D_Q_RANK, D_KV_RANK, D_ROPE, D_MIX)

V_ROWS = 80
VMEM_LIMIT = 56 * 1024 * 1024

TM = 512
TQ = 256
TQ_DENSE = 512
LOG2E = math.log2(math.e)
D_SCALE = (D_NOPE + D_ROPE) ** -0.5 * LOG2E
TK = 512
FLASH_SLOTS = 8
C_QROWS = 4
C_KROWS = 12


def _nt(a, b):
    return lax.dot_general(a, b, (((1,), (1,)), ((), ())), preferred_element_type=F32)


def _tn(a, b):
    return lax.dot_general(a, b, (((0,), (0,)), ((), ())), preferred_element_type=F32)


def _dot(a, b):
    return jnp.dot(a, b, preferred_element_type=F32)


T_AQ, T_AQR, T_BQ, T_BQR, T_CQ, T_DCQ, T_DCKV, T_AV, T_BV, T_CV, T_GATE, T_END = (
    0, 256, 512, 768, 1024, 1280, 1536, 1664, 1920, 2048, 2304, 3328)
S_AK, S_AKR, S_BK, S_BKR, S_CK, S_DCKV, S_KPE, S_KPER, S_END = (
    0, 256, 512, 640, 768, 1024, 1152, 1280, 1408)


def _proj_kernel(h_ref, g_ref, wt_ref, ws_ref, ta_ref, tb_ref, td_ref, sa_ref, sb_ref, sd_ref,
                 qg_ref, kvgc_ref, kvgr_ref, wqn_ref, wqp_ref, wqpr_ref, wkn_ref, place_ref, wvt_ref,
                 qa_ref, ka_ref, va_ref, qb_ref, kb_ref, vb_ref, qc_ref, kc_ref, vc_ref,
                 qd_ref, kd_ref, vd_ref, sg_ref):
    x = h_ref[0]
    tm = x.shape[0]
    ms = jnp.mean(x * x, axis=-1, keepdims=True)
    hn = ((x * lax.rsqrt(ms + EPS)) * g_ref[...]).astype(BF16)

    def zt(r0, r1):
        return _nt(wt_ref[r0:r1, :], hn)

    ones_rows = (lax.broadcasted_iota(jnp.int32, (V_ROWS - 64, tm), 0) == 0).astype(BF16)

    def store_values(ref, v, heads):
        for h in range(heads):
            ref[0, h, 0:64, :] = v[64 * h:64 * h + 64].astype(BF16)
            ref[0, h, 64:V_ROWS, :] = ones_rows

    qa = zt(T_AQ, T_AQR) * ta_ref[0] + zt(T_AQR, T_BQ) * ta_ref[1]
    for g in range(2 * A_HEADS):
        r = 32 * (g % 4)
        qa_ref[0, g] = jnp.zeros((128, tm), BF16)
        qa_ref[0, g, r:r + 32, :] = qa[32 * g:32 * g + 32].astype(BF16)
    store_values(va_ref, zt(T_AV, T_BV), A_HEADS)

    qb = zt(T_BQ, T_BQR) * tb_ref[0] + zt(T_BQR, T_CQ) * tb_ref[1]
    for h in range(B_HEADS):
        r = 64 * (h // 2)
        qb_ref[0, h] = jnp.zeros((128, tm), BF16)
        qb_ref[0, h, r:r + 64, :] = qb[64 * h:64 * h + 64].astype(BF16)
    store_values(vb_ref, zt(T_BV, T_CV), B_KV_HEADS)

    qc = zt(T_CQ, T_DCQ) * (C_HEAD_DIM ** -0.5)
    for h in range(C_HEADS):
        qc_ref[0, h] = jnp.zeros((256, tm), BF16)
        qc_ref[0, h, 64 * h:64 * h + 64, :] = qc[64 * h:64 * h + 64].astype(BF16)
    store_values(vc_ref, zt(T_CV, T_GATE), C_HEADS)

    cq = zt(T_DCQ, T_DCKV)
    cqn = ((cq * lax.rsqrt(jnp.mean(cq * cq, axis=0, keepdims=True) + EPS)) * qg_ref[...]).astype(BF16)
    for h in range(D_HEADS):
        qn = _dot(wqn_ref[h], cqn) * D_SCALE
        qpe = _dot(wqp_ref[h], cqn) * td_ref[0] + _dot(wqpr_ref[h], cqn) * td_ref[1]
        qd_ref[0, h, 0:64, :] = qn.astype(BF16)
        qd_ref[0, h, 64:96, :] = qpe.astype(BF16)
        qd_ref[0, h, 96:128, :] = jnp.zeros((32, tm), BF16)
    ckt = zt(T_DCKV, T_AV)
    cktn = ((ckt * lax.rsqrt(jnp.mean(ckt * ckt, axis=0, keepdims=True) + EPS)) * kvgc_ref[...]).astype(BF16)
    store_values(vd_ref, _dot(wvt_ref[...], cktn), D_HEADS)

    gt = zt(T_GATE, T_END)
    sg_ref[0] = gt * jax.nn.sigmoid(gt)

    zs = _dot(hn, ws_ref[...])
    ka_ref[0, 0] = (zs[:, S_AK:S_AKR] * sa_ref[0] + zs[:, S_AKR:S_BK] * sa_ref[1]).astype(BF16)
    kb_ref[0, 0] = (zs[:, S_BK:S_BKR] * sb_ref[0] + zs[:, S_BKR:S_CK] * sb_ref[1]).astype(BF16)
    kc_ref[0, 0] = zs[:, S_CK:S_DCKV].astype(BF16)
    ck = zs[:, S_DCKV:S_KPE]
    ckn = ((ck * lax.rsqrt(jnp.mean(ck * ck, axis=-1, keepdims=True) + EPS)) * kvgr_ref[...]).astype(BF16)
    kpe = (zs[:, S_KPE:S_KPER] * sd_ref[0] + zs[:, S_KPER:S_END] * sd_ref[1]).astype(BF16)
    kd_ref[0, 0] = (_dot(ckn, wkn_ref[...]) + _dot(kpe, place_ref[...])).astype(BF16)


def _rope_cs(s, dim):
    inv = 1.0 / (ROPE_THETA ** (jnp.arange(0, dim, 2, dtype=F32) / dim))
    ang = jnp.arange(s, dtype=F32)[:, None] * inv[None, :]
    cos, sin = jnp.cos(ang), jnp.sin(ang)
    return jnp.concatenate([cos, cos], -1), jnp.concatenate([-sin, sin], -1)


def _rope_tables(s):
    c32, s32 = _rope_cs(s, A_QK_DIM)
    c64, s64 = _rope_cs(s, B_HEAD_DIM)
    a_scale, b_scale, d_scale = A_QK_DIM ** -0.5 * LOG2E, B_HEAD_DIM ** -0.5, D_SCALE
    ta = jnp.stack([jnp.tile(c32, (1, 8)).T, jnp.tile(s32, (1, 8)).T]) * a_scale
    tb = jnp.stack([jnp.tile(c64, (1, 4)).T, jnp.tile(s64, (1, 4)).T]) * b_scale
    td = jnp.stack([c32.T, s32.T]) * d_scale
    sa = jnp.stack([jnp.tile(c32, (1, 8)), jnp.tile(s32, (1, 8))])
    sb = jnp.stack([jnp.tile(c64, (1, 2)), jnp.tile(s64, (1, 2))])
    pad = jnp.zeros((s, 128 - D_ROPE), F32)
    sd = jnp.stack([jnp.concatenate([c32, pad], -1), jnp.concatenate([s32, pad], -1)])
    return ta, tb, td, sa, sb, sd


def _rot_cols(w, d):
    k, n = w.shape
    return jnp.roll(w.reshape(k, n // d, d), d // 2, axis=-1).reshape(k, n)


def _prep_layer(w_in, d_w_uq, d_w_ukv):
    pts = [0]
    for n in IN_SIZES:
        pts.append(pts[-1] + n)
    (aq, ak, av, bq, bk, bv, cq, ck, cv, dcq, dckv, dkr, gate) = [w_in[:, pts[i]:pts[i + 1]] for i in range(13)]
    wt = jnp.concatenate([aq, _rot_cols(aq, A_QK_DIM), bq, _rot_cols(bq, B_HEAD_DIM), cq, dcq, dckv,
                          av, bv, cv, gate], axis=1).T.astype(BF16)
    zpad = jnp.zeros((D_MODEL, 128 - D_ROPE), F32)
    ws = jnp.concatenate([ak, _rot_cols(ak, A_QK_DIM), bk, _rot_cols(bk, B_HEAD_DIM), ck, dckv,
                          dkr, zpad, _rot_cols(dkr, D_ROPE), zpad], axis=1).astype(BF16)
    uq = d_w_uq.reshape(D_Q_RANK, D_HEADS, D_NOPE + D_ROPE)
    uq_n, uq_p = uq[:, :, :D_NOPE], uq[:, :, D_NOPE:]
    wqn = jnp.transpose(uq_n, (1, 2, 0))
    wqp = jnp.transpose(uq_p, (1, 2, 0))
    wqpr = jnp.transpose(jnp.roll(uq_p, D_ROPE // 2, axis=-1), (1, 2, 0))
    ukv = d_w_ukv.reshape(D_KV_RANK, D_HEADS, D_NOPE + D_V)
    wkn = jnp.pad(ukv[:, :, :D_NOPE], ((0, 0), (0, 0), (0, 128 - D_NOPE))).reshape(D_KV_RANK, D_HEADS * 128)
    place = np.zeros((128, D_HEADS, 128), np.float32)
    place[np.arange(D_ROPE), :, D_NOPE + np.arange(D_ROPE)] = 1.0
    place = jnp.asarray(place.reshape(128, D_HEADS * 128), BF16)
    wvt = ukv[:, :, D_NOPE:].reshape(D_KV_RANK, D_HEADS * D_V).T
    return (wt, ws, wqn.astype(BF16), wqp.astype(BF16), wqpr.astype(BF16), wkn.astype(BF16), place,
            wvt.astype(BF16))


def _const_spec(shape):
    nd = len(shape)
    return pl.BlockSpec(shape, lambda *_: (0,) * nd)


def _proj(h, norm_g, prep, tabs, q_norm_g, kv_norm_g):
    b, s, _ = h.shape
    tm = min(TM, s)
    wt, ws, wqn, wqp, wqpr, wkn, place, wvt = prep
    ta, tb, td, sa, sb, sd = tabs

    def tok(shape_tail):
        return pl.BlockSpec((1, 1, tm) + shape_tail, lambda i, j: (j, 0, i, 0))

    def chan(groups, rows):
        return pl.BlockSpec((1, groups, rows, tm), lambda i, j: (j, 0, 0, i))

    in_specs = [
        pl.BlockSpec((1, tm, D_MODEL), lambda i, j: (j, i, 0)),
        _const_spec((1, D_MODEL)), _const_spec(wt.shape), _const_spec(ws.shape),
        pl.BlockSpec((2, 256, tm), lambda i, j: (0, 0, i)),
        pl.BlockSpec((2, 256, tm), lambda i, j: (0, 0, i)),
        pl.BlockSpec((2, D_ROPE, tm), lambda i, j: (0, 0, i)),
        pl.BlockSpec((2, tm, 256), lambda i, j: (0, i, 0)),
        pl.BlockSpec((2, tm, 128), lambda i, j: (0, i, 0)),
        pl.BlockSpec((2, tm, 128), lambda i, j: (0, i, 0)),
        _const_spec((D_Q_RANK, 1)), _const_spec((D_KV_RANK, 1)), _const_spec((1, D_KV_RANK)),
        _const_spec(wqn.shape), _const_spec(wqp.shape), _const_spec(wqpr.shape),
        _const_spec(wkn.shape), _const_spec(place.shape), _const_spec(wvt.shape),
    ]
    out_shape = [
        jax.ShapeDtypeStruct((b, 8, 128, s), BF16),
        jax.ShapeDtypeStruct((b, 1, s, 256), BF16),
        jax.ShapeDtypeStruct((b, A_HEADS, V_ROWS, s), BF16),
        jax.ShapeDtypeStruct((b, B_HEADS, 128, s), BF16),
        jax.ShapeDtypeStruct((b, 1, s, 128), BF16),
        jax.ShapeDtypeStruct((b, B_KV_HEADS, V_ROWS, s), BF16),
        jax.ShapeDtypeStruct((b, C_HEADS, 256, s), BF16),
        jax.ShapeDtypeStruct((b, 1, s, 256), BF16),
        jax.ShapeDtypeStruct((b, C_HEADS, V_ROWS, s), BF16),
        jax.ShapeDtypeStruct((b, D_HEADS, 128, s), BF16),
        jax.ShapeDtypeStruct((b, 1, s, D_HEADS * 128), BF16),
        jax.ShapeDtypeStruct((b, D_HEADS, V_ROWS, s), BF16),
        jax.ShapeDtypeStruct((b, D_MIX, s), F32),
    ]
    out_specs = [
        chan(8, 128), tok((256,)), chan(A_HEADS, V_ROWS),
        chan(B_HEADS, 128), tok((128,)), chan(B_KV_HEADS, V_ROWS),
        chan(C_HEADS, 256), tok((256,)), chan(C_HEADS, V_ROWS),
        chan(D_HEADS, 128), tok((D_HEADS * 128,)), chan(D_HEADS, V_ROWS),
        pl.BlockSpec((1, D_MIX, tm), lambda i, j: (j, 0, i)),
    ]
    return pl.pallas_call(
        _proj_kernel,
        grid=(s // tm, b),
        in_specs=in_specs, out_specs=out_specs, out_shape=out_shape,
        compiler_params=pltpu.CompilerParams(
            dimension_semantics=("arbitrary", "arbitrary"), vmem_limit_bytes=VMEM_LIMIT),
        name="proj",
    )(h, norm_g.reshape(1, D_MODEL), wt, ws, ta, tb, td, sa, sb, sd,
      q_norm_g.reshape(D_Q_RANK, 1), kv_norm_g.reshape(D_KV_RANK, 1), kv_norm_g.reshape(1, D_KV_RANK),
      wqn, wqp, wqpr, wkn, place, wvt)


def _flash_kernel(k_ref, q_ref, v_ref, o_ref, s_buf, p_buf, *, tk, nk):
    q = q_ref[0, 0]
    tq = q.shape[1]

    def scores(j, slot):
        off = pl.multiple_of(j * tk, tk)
        s = _dot(k_ref[0, 0, pl.ds(off, tk), :], q)
        s_buf[slot] = s
        return jnp.max(s, axis=0, keepdims=True)

    def softmax(slot, m, cmax):
        m_new = jnp.maximum(m, cmax)
        p_buf[slot] = jnp.exp2(s_buf[slot] - m_new).astype(BF16)
        return m_new, jnp.exp2(m - m_new)

    def values(j, slot, acc, alpha):
        off = pl.multiple_of(j * tk, tk)
        return acc * alpha + _dot(v_ref[0, 0, :, pl.ds(off, tk)], p_buf[slot])

    slots = s_buf.shape[0]

    def body(jj, carry):
        m, alpha, acc, cmax = carry
        for u in range(slots):
            j = slots * jj + u
            cmax_next = scores(jnp.minimum(j + 1, nk - 1), (u + 1) % slots)
            m, alpha_new = softmax(u, m, cmax)
            acc = values(jnp.maximum(j - 1, 0), (u - 1) % slots, acc, alpha)
            alpha, cmax = alpha_new, cmax_next
        return m, alpha, acc, cmax

    cmax0 = scores(0, 0)
    p_buf[slots - 1] = jnp.zeros(p_buf.shape[1:], BF16)
    m0 = jnp.full((1, tq), NEG_INF, F32)
    a0 = jnp.ones((1, tq), F32)
    acc0 = jnp.zeros((V_ROWS, tq), F32)
    m, alpha, acc, _ = lax.fori_loop(0, nk // slots, body, (m0, a0, acc0, cmax0))
    acc = values(nk - 1, slots - 1, acc, alpha)
    o_ref[0, 0] = acc[0:64] / acc[64:65]


def _flash(k, qt, vt, k_group, v_group):
    b, g, dk, s = qt.shape
    tq, tk = min(TQ_DENSE, s), min(TK, s)
    nk = s // tk
    slots = math.gcd(nk, FLASH_SLOTS)
    assert slots >= 2
    return pl.pallas_call(
        functools.partial(_flash_kernel, tk=tk, nk=nk),
        grid=(b, g, s // tq),
        scratch_shapes=[pltpu.VMEM((slots, tk, tq), F32), pltpu.VMEM((slots, tk, tq), BF16)],
        in_specs=[
            pl.BlockSpec((1, 1, s, dk), lambda bi, gi, i: (bi,) + k_group(gi)),
            pl.BlockSpec((1, 1, dk, tq), lambda bi, gi, i: (bi, gi, 0, i)),
            pl.BlockSpec((1, 1, V_ROWS, s), lambda bi, gi, i: (bi, v_group(gi), 0, 0)),
        ],
        out_specs=pl.BlockSpec((1, 1, 64, tq), lambda bi, gi, i: (bi, gi, 0, i)),
        out_shape=jax.ShapeDtypeStruct((b, g, 64, s), F32),
        compiler_params=pltpu.CompilerParams(
            dimension_semantics=("arbitrary", "arbitrary", "arbitrary"), vmem_limit_bytes=VMEM_LIMIT),
        name="flash",
    )(k, qt, vt)


def _window_kernel(sink_ref, k_ref, q_ref, v_ref, o_ref, *, s_len, win):
    h, i = pl.program_id(1), pl.program_id(2)
    q = q_ref[0, 0]
    tq = q.shape[1]
    q0 = i * tq
    start = pl.multiple_of(jnp.clip(q0 - B_WINDOW, 0, s_len - win), 128)
    k = k_ref[0, 0, pl.ds(start, win), :]
    s = _dot(k, q)
    kpos = start + lax.broadcasted_iota(jnp.int32, (win, tq), 0)
    qpos = q0 + lax.broadcasted_iota(jnp.int32, (win, tq), 1)
    s = jnp.where(jnp.abs(kpos - qpos) <= B_WINDOW, s, NEG_INF)
    sink = sink_ref[h]
    m = jnp.maximum(jnp.max(s, axis=0, keepdims=True), sink)
    p = jnp.exp(s - m).astype(BF16)
    acc = _dot(v_ref[0, 0, :, pl.ds(start, win)], p)
    o_ref[0, 0] = acc[0:64] / (acc[64:65] + jnp.exp(sink - m))


def _window(k, qt, vt, sink):
    b, g, dk, s = qt.shape
    tq = min(TQ, s)
    win = tq + 2 * B_WINDOW
    return pl.pallas_call(
        functools.partial(_window_kernel, s_len=s, win=win),
        grid=(b, g, s // tq),
        in_specs=[
            pl.BlockSpec(memory_space=pltpu.SMEM),
            pl.BlockSpec((1, 1, s, dk), lambda bi, gi, i: (bi, 0, 0, 0)),
            pl.BlockSpec((1, 1, dk, tq), lambda bi, gi, i: (bi, gi, 0, i)),
            pl.BlockSpec((1, 1, V_ROWS, s), lambda bi, gi, i: (bi, gi // (B_HEADS // B_KV_HEADS), 0, 0)),
        ],
        out_specs=pl.BlockSpec((1, 1, 64, tq), lambda bi, gi, i: (bi, gi, 0, i)),
        out_shape=jax.ShapeDtypeStruct((b, g, 64, s), F32),
        compiler_params=pltpu.CompilerParams(
            dimension_semantics=("arbitrary", "arbitrary", "arbitrary"), vmem_limit_bytes=VMEM_LIMIT),
        name="window",
    )(sink.astype(F32), k, qt, vt)


def _nbr_kernel(bias_ref, k_ref, q_ref, v_ref, o_ref, *, s_len, win):
    i = pl.program_id(2)
    q = q_ref[0, 0]
    tq = q.shape[1]
    start = pl.multiple_of(jnp.clip(i * tq - (C_MAX_ROWS // 2) * GRID_W, 0, s_len - win), 256)
    k = k_ref[0, 0, pl.ds(start, win), :]
    s = _dot(k, q) + bias_ref[0, 0]
    m = jnp.max(s, axis=0, keepdims=True)
    p = jnp.exp(s - m).astype(BF16)
    acc = _dot(v_ref[0, 0, :, pl.ds(start, win)], p)
    o_ref[0, 0] = acc[0:64] / acc[64:65]


def _nbr_bias(rpb, rows):
    col = np.arange(GRID_W)
    cs = np.clip(col - C_WIN_COLS // 2, 0, GRID_W - C_WIN_COLS)
    col_ok = (col[:, None] >= cs[None, :]) & (col[:, None] < cs[None, :] + C_WIN_COLS)
    dc = np.clip(col[:, None] - col[None, :], -(C_WIN_COLS - 1), C_WIN_COLS - 1) + (C_WIN_COLS - 1)
    rp = rpb.astype(F32)
    by_cols = jnp.zeros(rp.shape[:2] + dc.shape, F32)
    for b in range(rp.shape[2]):
        by_cols = by_cols + jnp.where((dc == b)[None, None], rp[:, :, b][:, :, None, None], 0.0)
    kr = min(C_MAX_ROWS, rows)
    out = []
    for r0 in (0, C_QROWS, rows - C_QROWS):
        rk0 = min(max(r0 - C_MAX_ROWS // 2, 0), rows - C_KROWS)
        rq = r0 + np.arange(C_QROWS)
        rk = rk0 + np.arange(C_KROWS)
        rs = np.clip(rq - kr // 2, 0, rows - kr)
        row_ok = (rk[:, None] >= rs[None, :]) & (rk[:, None] < rs[None, :] + kr)
        dr = np.clip(rk[:, None] - rq[None, :] + (C_MAX_ROWS - 1), 0, 2 * C_MAX_ROWS - 2)
        blocks = jnp.stack([jnp.stack([by_cols[:, dr[a, c]] for c in range(C_QROWS)], axis=2)
                            for a in range(C_KROWS)], axis=1)
        ok = row_ok[:, None, :, None] & col_ok[None, :, None, :]
        bias = jnp.where(ok[None], blocks, NEG_INF)
        out.append(bias.reshape(rp.shape[0], C_KROWS * GRID_W, C_QROWS * GRID_W))
    return jnp.stack(out)


def _nbr(k, qt, vt, rpb):
    b, g, dk, s = qt.shape
    tq = C_QROWS * GRID_W
    win = C_KROWS * GRID_W
    nq = s // tq
    bias = _nbr_bias(rpb, s // GRID_W)

    def variant(i):
        return jnp.where(i == 0, 0, jnp.where(i == nq - 1, 2, 1))

    return pl.pallas_call(
        functools.partial(_nbr_kernel, s_len=s, win=win),
        grid=(b, g, nq),
        in_specs=[
            pl.BlockSpec((1, 1, win, tq), lambda bi, gi, i: (variant(i), gi, 0, 0)),
            pl.BlockSpec((1, 1, s, dk), lambda bi, gi, i: (bi, 0, 0, 0)),
            pl.BlockSpec((1, 1, dk, tq), lambda bi, gi, i: (bi, gi, 0, i)),
            pl.BlockSpec((1, 1, V_ROWS, s), lambda bi, gi, i: (bi, gi, 0, 0)),
        ],
        out_specs=pl.BlockSpec((1, 1, 64, tq), lambda bi, gi, i: (bi, gi, 0, i)),
        out_shape=jax.ShapeDtypeStruct((b, g, 64, s), F32),
        compiler_params=pltpu.CompilerParams(
            dimension_semantics=("arbitrary", "arbitrary", "arbitrary"), vmem_limit_bytes=VMEM_LIMIT),
        name="nbr",
    )(bias, k, qt, vt)


def _out_kernel(oa_ref, ob_ref, oc_ref, od_ref, sg_ref, h_ref, p_ref, lam_ref, sub_ref,
                wo_ref, pg_ref, wg_ref, wp_ref, fg_ref, y_ref, *, lam_init, final):
    lv = lam_ref[...]
    lam = (jnp.exp(jnp.sum(lv[0:1] * lv[1:2], axis=-1, keepdims=True))
           - jnp.exp(jnp.sum(lv[2:3] * lv[3:4], axis=-1, keepdims=True)) + lam_init)
    parts = []
    for h in range(A_HEADS):
        o = oa_ref[0, 2 * h] - lam * oa_ref[0, 2 * h + 1]
        on = (o * lax.rsqrt(jnp.mean(o * o, axis=0, keepdims=True) + EPS)) * sub_ref[...]
        parts.append(on * (1.0 - lam_init))
    for ref in (ob_ref, oc_ref, od_ref):
        for h in range(4):
            parts.append(ref[0, h])
    mix = (jnp.concatenate(parts, axis=0) * sg_ref[0]).astype(BF16)
    h1 = h_ref[0] + _tn(mix, wo_ref[...])
    hn = ((h1 * lax.rsqrt(jnp.mean(h1 * h1, axis=-1, keepdims=True) + EPS)) * pg_ref[...]).astype(BF16)
    gate = jax.nn.sigmoid(_dot(hn, wg_ref[...]))
    h2 = h1 + gate * _dot(p_ref[0].astype(BF16), wp_ref[...])
    if final:
        h2 = (h2 * lax.rsqrt(jnp.mean(h2 * h2, axis=-1, keepdims=True) + EPS)) * fg_ref[...]
    y_ref[0] = h2


def _out(oa, ob, oc, od, sg, h, p, a_lambda, subln_g, w_out, ple_norm_g, w_ple_gate, w_ple_proj,
         final_norm_g, lam_init, final):
    b, s, _ = h.shape
    tm = min(TM, s)

    def heads(n):
        return pl.BlockSpec((1, n, 64, tm), lambda bi, i: (bi, 0, 0, i))

    return pl.pallas_call(
        functools.partial(_out_kernel, lam_init=lam_init, final=final),
        grid=(b, s // tm),
        in_specs=[
            heads(8), heads(4), heads(4), heads(4),
            pl.BlockSpec((1, D_MIX, tm), lambda bi, i: (bi, 0, i)),
            pl.BlockSpec((1, tm, D_MODEL), lambda bi, i: (bi, i, 0)),
            pl.BlockSpec((1, tm, D_PLE), lambda bi, i: (bi, i, 0)),
            _const_spec((4, A_QK_DIM)), _const_spec((A_V_DIM, 1)),
            _const_spec((D_MIX, D_MODEL)), _const_spec((1, D_MODEL)),
            _const_spec((D_MODEL, D_MODEL)), _const_spec((D_PLE, D_MODEL)), _const_spec((1, D_MODEL)),
        ],
        out_specs=pl.BlockSpec((1, tm, D_MODEL), lambda bi, i: (bi, i, 0)),
        out_shape=jax.ShapeDtypeStruct((b, s, D_MODEL), F32),
        compiler_params=pltpu.CompilerParams(
            dimension_semantics=("arbitrary", "arbitrary"), vmem_limit_bytes=VMEM_LIMIT),
        name="out",
    )(oa, ob, oc, od, sg, h, p, a_lambda.astype(F32), subln_g.reshape(A_V_DIM, 1).astype(F32),
      w_out.astype(BF16), ple_norm_g.reshape(1, D_MODEL), w_ple_gate.astype(BF16), w_ple_proj.astype(BF16),
      final_norm_g.reshape(1, D_MODEL))


def _trunk(x, p, norm_g, w_in, a_lambda, a_subln_g, b_sink, c_rpb, d_q_norm_g, d_kv_norm_g,
           d_w_uq, d_w_ukv, w_out, ple_norm_g, w_ple_gate, w_ple_proj, final_norm_g):
    depth = w_in.shape[0]
    s = x.shape[1]
    tabs = _rope_tables(s)
    h = x
    for i in range(depth):
        prep = _prep_layer(w_in[i], d_w_uq[i], d_w_ukv[i])
        (qa, ka, va, qb, kb, vb, qc, kc, vc, qd, kd, vd, sg) = _proj(
            h, norm_g[i], prep, tabs, d_q_norm_g[i], d_kv_norm_g[i])
        oa = _flash(ka, qa, va, lambda g: (0, 0, g // 4), lambda g: g // 2)
        od = _flash(kd, qd, vd, lambda g: (0, 0, g), lambda g: g)
        ob = _window(kb, qb, vb, b_sink[i])
        oc = _nbr(kc, qc, vc, c_rpb[i])
        lam_init = 0.8 - 0.6 * math.exp(-0.3 * i)
        h = _out(oa, ob, oc, od, sg, h, p[i], a_lambda[i], a_subln_g[i], w_out[i], ple_norm_g[i],
                 w_ple_gate[i], w_ple_proj[i], final_norm_g, lam_init, i == depth - 1)
    return h


def kernel(x_prompt, x_sample, p_prompt, p_sample, norm_g, w_in, a_lambda, a_subln_g, b_sink, c_rpb,
           d_q_norm_g, d_kv_norm_g, d_w_uq, d_w_ukv, w_out, ple_norm_g, w_ple_gate, w_ple_proj,
           final_norm_g):
    weights = (norm_g, w_in, a_lambda, a_subln_g, b_sink, c_rpb, d_q_norm_g, d_kv_norm_g, d_w_uq, d_w_ukv,
               w_out, ple_norm_g, w_ple_gate, w_ple_proj, final_norm_g)
    return (_trunk(x_prompt, p_prompt, *weights), _trunk(x_sample, p_sample, *weights))
```

```python
import functools
import math

import jax
import jax.numpy as jnp
import numpy as np
from jax import lax
from jax.experimental import pallas as pl
from jax.experimental.pallas import tpu as pltpu

F32 = jnp.float32
BF16 = jnp.bfloat16

D_MODEL = 1024
D_MIX = 1024
D_PLE = 256
GRID_W = 64
ROPE_THETA = 10000.0
EPS = 1e-6
NEG_INF = -1e30

A_HEADS, A_QK_DIM, A_V_DIM = 4, 32, 64
B_HEADS, B_KV_HEADS, B_HEAD_DIM, B_WINDOW = 4, 2, 64, 128
C_HEADS, C_HEAD_DIM, C_MAX_ROWS, C_WIN_COLS = 4, 64, 8, 16
D_HEADS, D_Q_RANK, D_KV_RANK, D_NOPE, D_ROPE, D_V = 4, 256, 128, 64, 32, 64

IN_SIZES = (256, 256, 256, 256, 128, 128, 256, 256, 256, D_Q_RANK, D_KV_RANK, D_ROPE, D_MIX)

V_ROWS = 80
VMEM_LIMIT = 56 * 1024 * 1024

TM = 512
TQ = 256
TQ_DENSE = 512
LOG2E = math.log2(math.e)
D_SCALE = (D_NOPE + D_ROPE) ** -0.5 * LOG2E
TK = 512
FLASH_SLOTS = 16
C_QROWS = 4
C_KROWS = 12


def _nt(a, b):
    return lax.dot_general(a, b, (((1,), (1,)), ((), ())), preferred_element_type=F32)


def _tn(a, b):
    return lax.dot_general(a, b, (((0,), (0,)), ((), ())), preferred_element_type=F32)


def _dot(a, b):
    return jnp.dot(a, b, preferred_element_type=F32)


T_AQ, T_AQR, T_BQ, T_BQR, T_CQ, T_DCQ, T_DCKV, T_AV, T_BV, T_CV, T_GATE, T_END = (
    0, 256, 512, 768, 1024, 1280, 1536, 1664, 1920, 2048, 2304, 3328)
S_AK, S_AKR, S_BK, S_BKR, S_CK, S_DCKV, S_KPE, S_KPER, S_END = (
    0, 256, 512, 640, 768, 1024, 1152, 1280, 1408)


def _proj_kernel(h_ref, g_ref, wt_ref, ws_ref, ta_ref, tb_ref, td_ref, sa_ref, sb_ref, sd_ref,
                 qg_ref, kvgc_ref, kvgr_ref, wqn_ref, wqp_ref, wqpr_ref, wkn_ref, place_ref, wvt_ref,
                 qa_ref, ka_ref, va_ref, qb_ref, kb_ref, vb_ref, qc_ref, kc_ref, vc_ref,
                 qd_ref, kd_ref, vd_ref, sg_ref):
    x = h_ref[0]
    tm = x.shape[0]
    ms = jnp.mean(x * x, axis=-1, keepdims=True)
    hn = ((x * lax.rsqrt(ms + EPS)) * g_ref[...]).astype(BF16)

    def zt(r0, r1):
        return _nt(wt_ref[r0:r1, :], hn)

    ones_rows = (lax.broadcasted_iota(jnp.int32, (V_ROWS - 64, tm), 0) == 0).astype(BF16)

    def store_values(ref, v, heads):
        for h in range(heads):
            ref[0, h, 0:64, :] = v[64 * h:64 * h + 64].astype(BF16)
            ref[0, h, 64:V_ROWS, :] = ones_rows

    qa = zt(T_AQ, T_AQR) * ta_ref[0] + zt(T_AQR, T_BQ) * ta_ref[1]
    for g in range(2 * A_HEADS):
        r = 32 * (g % 4)
        qa_ref[0, g] = jnp.zeros((128, tm), BF16)
        qa_ref[0, g, r:r + 32, :] = qa[32 * g:32 * g + 32].astype(BF16)
    store_values(va_ref, zt(T_AV, T_BV), A_HEADS)

    qb = zt(T_BQ, T_BQR) * tb_ref[0] + zt(T_BQR, T_CQ) * tb_ref[1]
    for h in range(B_HEADS):
        r = 64 * (h // 2)
        qb_ref[0, h] = jnp.zeros((128, tm), BF16)
        qb_ref[0, h, r:r + 64, :] = qb[64 * h:64 * h + 64].astype(BF16)
    store_values(vb_ref, zt(T_BV, T_CV), B_KV_HEADS)

    qc = zt(T_CQ, T_DCQ) * (C_HEAD_DIM ** -0.5)
    for h in range(C_HEADS):
        r = 64 * (h % 2)
        qc_ref[0, h] = jnp.zeros((128, tm), BF16)
        qc_ref[0, h, r:r + 64, :] = qc[64 * h:64 * h + 64].astype(BF16)
    store_values(vc_ref, zt(T_CV, T_GATE), C_HEADS)

    cq = zt(T_DCQ, T_DCKV)
    cqn = ((cq * lax.rsqrt(jnp.mean(cq * cq, axis=0, keepdims=True) + EPS)) * qg_ref[...]).astype(BF16)
    for h in range(D_HEADS):
        qn = _dot(wqn_ref[h], cqn) * D_SCALE
        qpe = _dot(wqp_ref[h], cqn) * td_ref[0] + _dot(wqpr_ref[h], cqn) * td_ref[1]
        qd_ref[0, h, 0:64, :] = qn.astype(BF16)
        qd_ref[0, h, 64:96, :] = qpe.astype(BF16)
        qd_ref[0, h, 96:128, :] = jnp.zeros((32, tm), BF16)
    ckt = zt(T_DCKV, T_AV)
    cktn = ((ckt * lax.rsqrt(jnp.mean(ckt * ckt, axis=0, keepdims=True) + EPS)) * kvgc_ref[...]).astype(BF16)
    store_values(vd_ref, _dot(wvt_ref[...], cktn), D_HEADS)

    gt = zt(T_GATE, T_END)
    sg_ref[0] = gt * jax.nn.sigmoid(gt)

    zs = _dot(hn, ws_ref[...])
    ka_ref[0, 0] = (zs[:, S_AK:S_AKR] * sa_ref[0] + zs[:, S_AKR:S_BK] * sa_ref[1]).astype(BF16)
    kb_ref[0, 0] = (zs[:, S_BK:S_BKR] * sb_ref[0] + zs[:, S_BKR:S_CK] * sb_ref[1]).astype(BF16)
    kc_ref[0, 0] = zs[:, S_CK:S_DCKV].astype(BF16)
    ck = zs[:, S_DCKV:S_KPE]
    ckn = ((ck * lax.rsqrt(jnp.mean(ck * ck, axis=-1, keepdims=True) + EPS)) * kvgr_ref[...]).astype(BF16)
    kpe = (zs[:, S_KPE:S_KPER] * sd_ref[0] + zs[:, S_KPER:S_END] * sd_ref[1]).astype(BF16)
    kd_ref[0, 0] = (_dot(ckn, wkn_ref[...]) + _dot(kpe, place_ref[...])).astype(BF16)


def _rope_cs(s, dim):
    inv = 1.0 / (ROPE_THETA ** (jnp.arange(0, dim, 2, dtype=F32) / dim))
    ang = jnp.arange(s, dtype=F32)[:, None] * inv[None, :]
    cos, sin = jnp.cos(ang), jnp.sin(ang)
    return jnp.concatenate([cos, cos], -1), jnp.concatenate([-sin, sin], -1)


def _rope_tables(s):
    c32, s32 = _rope_cs(s, A_QK_DIM)
    c64, s64 = _rope_cs(s, B_HEAD_DIM)
    a_scale, b_scale, d_scale = A_QK_DIM ** -0.5 * LOG2E, B_HEAD_DIM ** -0.5, D_SCALE
    ta = jnp.stack([jnp.tile(c32, (1, 8)).T, jnp.tile(s32, (1, 8)).T]) * a_scale
    tb = jnp.stack([jnp.tile(c64, (1, 4)).T, jnp.tile(s64, (1, 4)).T]) * b_scale
    td = jnp.stack([c32.T, s32.T]) * d_scale
    sa = jnp.stack([jnp.tile(c32, (1, 8)), jnp.tile(s32, (1, 8))])
    sb = jnp.stack([jnp.tile(c64, (1, 2)), jnp.tile(s64, (1, 2))])
    pad = jnp.zeros((s, 128 - D_ROPE), F32)
    sd = jnp.stack([jnp.concatenate([c32, pad], -1), jnp.concatenate([s32, pad], -1)])
    return ta, tb, td, sa, sb, sd


def _rot_cols(w, d):
    k, n = w.shape
    return jnp.roll(w.reshape(k, n // d, d), d // 2, axis=-1).reshape(k, n)


def _prep_layer(w_in, d_w_uq, d_w_ukv):
    pts = [0]
    for n in IN_SIZES:
        pts.append(pts[-1] + n)
    (aq, ak, av, bq, bk, bv, cq, ck, cv, dcq, dckv, dkr, gate) = [w_in[:, pts[i]:pts[i + 1]] for i in range(13)]
    wt = jnp.concatenate([aq, _rot_cols(aq, A_QK_DIM), bq, _rot_cols(bq, B_HEAD_DIM), cq, dcq, dckv,
                          av, bv, cv, gate], axis=1).T.astype(BF16)
    zpad = jnp.zeros((D_MODEL, 128 - D_ROPE), F32)
    ws = jnp.concatenate([ak, _rot_cols(ak, A_QK_DIM), bk, _rot_cols(bk, B_HEAD_DIM), ck, dckv,
                          dkr, zpad, _rot_cols(dkr, D_ROPE), zpad], axis=1).astype(BF16)
    uq = d_w_uq.reshape(D_Q_RANK, D_HEADS, D_NOPE + D_ROPE)
    uq_n, uq_p = uq[:, :, :D_NOPE], uq[:, :, D_NOPE:]
    wqn = jnp.transpose(uq_n, (1, 2, 0))
    wqp = jnp.transpose(uq_p, (1, 2, 0))
    wqpr = jnp.transpose(jnp.roll(uq_p, D_ROPE // 2, axis=-1), (1, 2, 0))
    ukv = d_w_ukv.reshape(D_KV_RANK, D_HEADS, D_NOPE + D_V)
    wkn = jnp.pad(ukv[:, :, :D_NOPE], ((0, 0), (0, 0), (0, 128 - D_NOPE))).reshape(D_KV_RANK, D_HEADS * 128)
    place = np.zeros((128, D_HEADS, 128), np.float32)
    place[np.arange(D_ROPE), :, D_NOPE + np.arange(D_ROPE)] = 1.0
    place = jnp.asarray(place.reshape(128, D_HEADS * 128), BF16)
    wvt = ukv[:, :, D_NOPE:].reshape(D_KV_RANK, D_HEADS * D_V).T
    return (wt, ws, wqn.astype(BF16), wqp.astype(BF16), wqpr.astype(BF16), wkn.astype(BF16), place,
            wvt.astype(BF16))


def _const_spec(shape):
    nd = len(shape)
    return pl.BlockSpec(shape, lambda *_: (0,) * nd)


def _proj(h, norm_g, prep, tabs, q_norm_g, kv_norm_g):
    b, s, _ = h.shape
    tm = min(TM, s)
    wt, ws, wqn, wqp, wqpr, wkn, place, wvt = prep
    ta, tb, td, sa, sb, sd = tabs

    def tok(shape_tail):
        return pl.BlockSpec((1, 1, tm) + shape_tail, lambda i, j: (j, 0, i, 0))

    def chan(groups, rows):
        return pl.BlockSpec((1, groups, rows, tm), lambda i, j: (j, 0, 0, i))

    in_specs = [
        pl.BlockSpec((1, tm, D_MODEL), lambda i, j: (j, i, 0)),
        _const_spec((1, D_MODEL)), _const_spec(wt.shape), _const_spec(ws.shape),
        pl.BlockSpec((2, 256, tm), lambda i, j: (0, 0, i)),
        pl.BlockSpec((2, 256, tm), lambda i, j: (0, 0, i)),
        pl.BlockSpec((2, D_ROPE, tm), lambda i, j: (0, 0, i)),
        pl.BlockSpec((2, tm, 256), lambda i, j: (0, i, 0)),
        pl.BlockSpec((2, tm, 128), lambda i, j: (0, i, 0)),
        pl.BlockSpec((2, tm, 128), lambda i, j: (0, i, 0)),
        _const_spec((D_Q_RANK, 1)), _const_spec((D_KV_RANK, 1)), _const_spec((1, D_KV_RANK)),
        _const_spec(wqn.shape), _const_spec(wqp.shape), _const_spec(wqpr.shape),
        _const_spec(wkn.shape), _const_spec(place.shape), _const_spec(wvt.shape),
    ]
    out_shape = [
        jax.ShapeDtypeStruct((b, 8, 128, s), BF16),
        jax.ShapeDtypeStruct((b, 1, s, 256), BF16),
        jax.ShapeDtypeStruct((b, A_HEADS, V_ROWS, s), BF16),
        jax.ShapeDtypeStruct((b, B_HEADS, 128, s), BF16),
        jax.ShapeDtypeStruct((b, 1, s, 128), BF16),
        jax.ShapeDtypeStruct((b, B_KV_HEADS, V_ROWS, s), BF16),
        jax.ShapeDtypeStruct((b, C_HEADS, 128, s), BF16),
        jax.ShapeDtypeStruct((b, 1, s, 256), BF16),
        jax.ShapeDtypeStruct((b, C_HEADS, V_ROWS, s), BF16),
        jax.ShapeDtypeStruct((b, D_HEADS, 128, s), BF16),
        jax.ShapeDtypeStruct((b, 1, s, D_HEADS * 128), BF16),
        jax.ShapeDtypeStruct((b, D_HEADS, V_ROWS, s), BF16),
        jax.ShapeDtypeStruct((b, D_MIX, s), F32),
    ]
    out_specs = [
        chan(8, 128), tok((256,)), chan(A_HEADS, V_ROWS),
        chan(B_HEADS, 128), tok((128,)), chan(B_KV_HEADS, V_ROWS),
        chan(C_HEADS, 128), tok((256,)), chan(C_HEADS, V_ROWS),
        chan(D_HEADS, 128), tok((D_HEADS * 128,)), chan(D_HEADS, V_ROWS),
        pl.BlockSpec((1, D_MIX, tm), lambda i, j: (j, 0, i)),
    ]
    return pl.pallas_call(
        _proj_kernel,
        grid=(s // tm, b),
        in_specs=in_specs, out_specs=out_specs, out_shape=out_shape,
        compiler_params=pltpu.CompilerParams(
            dimension_semantics=("arbitrary", "arbitrary"), vmem_limit_bytes=VMEM_LIMIT),
        name="proj",
    )(h, norm_g.reshape(1, D_MODEL), wt, ws, ta, tb, td, sa, sb, sd,
      q_norm_g.reshape(D_Q_RANK, 1), kv_norm_g.reshape(D_KV_RANK, 1), kv_norm_g.reshape(1, D_KV_RANK),
      wqn, wqp, wqpr, wkn, place, wvt)


def _flash_kernel(k_ref, q_ref, v_ref, o_ref, s_buf, p_buf, pv_buf, *, tk, nk):
    q = q_ref[0, 0]
    tq = q.shape[1]

    def scores(j, slot):
        off = pl.multiple_of(j * tk, tk)
        s = _dot(k_ref[0, 0, pl.ds(off, tk), :], q)
        s_buf[slot] = s
        return jnp.max(s, axis=0, keepdims=True)

    def softmax(slot, m, cmax):
        m_new = jnp.maximum(m, cmax)
        p_buf[slot] = jnp.exp2(s_buf[slot] - m_new).astype(BF16)
        return m_new, jnp.exp2(m - m_new)

    def values(j, slot, acc, alpha):
        off = pl.multiple_of(j * tk, tk)
        pv_buf[...] = _dot(v_ref[0, 0, :, pl.ds(off, tk)], p_buf[slot])
        return acc * alpha + pv_buf[...]

    slots = s_buf.shape[0]

    def body(jj, carry):
        m, alpha, acc, cmax = carry
        for u in range(slots):
            j = slots * jj + u
            cmax_next = scores(jnp.minimum(j + 1, nk - 1), (u + 1) % slots)
            m, alpha_new = softmax(u, m, cmax)
            acc = values(jnp.maximum(j - 1, 0), (u - 1) % slots, acc, alpha)
            alpha, cmax = alpha_new, cmax_next
        return m, alpha, acc, cmax

    cmax0 = scores(0, 0)
    p_buf[slots - 1] = jnp.zeros(p_buf.shape[1:], BF16)
    m0 = jnp.full((1, tq), NEG_INF, F32)
    a0 = jnp.ones((1, tq), F32)
    acc0 = jnp.zeros((V_ROWS, tq), F32)
    m, alpha, acc, _ = lax.fori_loop(0, nk // slots, body, (m0, a0, acc0, cmax0))
    acc = values(nk - 1, slots - 1, acc, alpha)
    o_ref[0, 0] = acc[0:64] / acc[64:65]


def _flash(k, qt, vt, k_group, v_group):
    b, g, dk, s = qt.shape
    tq, tk = min(TQ_DENSE, s), min(TK, s)
    nk = s // tk
    slots = math.gcd(nk, FLASH_SLOTS)
    assert slots >= 2
    return pl.pallas_call(
        functools.partial(_flash_kernel, tk=tk, nk=nk),
        grid=(b, g, s // tq),
        scratch_shapes=[pltpu.VMEM((slots, tk, tq), F32), pltpu.VMEM((slots, tk, tq), BF16),
                        pltpu.VMEM((V_ROWS, tq), F32)],
        in_specs=[
            pl.BlockSpec((1, 1, s, dk), lambda bi, gi, i: (bi,) + k_group(gi)),
            pl.BlockSpec((1, 1, dk, tq), lambda bi, gi, i: (bi, gi, 0, i)),
            pl.BlockSpec((1, 1, V_ROWS, s), lambda bi, gi, i: (bi, v_group(gi), 0, 0)),
        ],
        out_specs=pl.BlockSpec((1, 1, 64, tq), lambda bi, gi, i: (bi, gi, 0, i)),
        out_shape=jax.ShapeDtypeStruct((b, g, 64, s), F32),
        compiler_params=pltpu.CompilerParams(
            dimension_semantics=("arbitrary", "arbitrary", "arbitrary"), vmem_limit_bytes=VMEM_LIMIT),
        name="flash",
    )(k, qt, vt)


def _window_kernel(sink_ref, k_ref, q_ref, v_ref, o_ref, *, s_len, win):
    i = pl.program_id(1)
    tq = q_ref.shape[3]
    q0 = i * tq
    start = pl.multiple_of(jnp.clip(q0 - B_WINDOW, 0, s_len - win), 128)
    k = k_ref[0, 0, pl.ds(start, win), :]
    kpos = start + lax.broadcasted_iota(jnp.int32, (win, tq), 0)
    qpos = q0 + lax.broadcasted_iota(jnp.int32, (win, tq), 1)
    in_band = jnp.abs(kpos - qpos) <= B_WINDOW
    scores = [_dot(k, q_ref[0, h]) for h in range(B_HEADS)]
    for h in range(B_HEADS):
        s = jnp.where(in_band, scores[h], NEG_INF)
        sink = sink_ref[h]
        m = jnp.maximum(jnp.max(s, axis=0, keepdims=True), sink)
        p = jnp.exp(s - m).astype(BF16)
        acc = _dot(v_ref[0, h // (B_HEADS // B_KV_HEADS), :, pl.ds(start, win)], p)
        o_ref[0, h] = acc[0:64] / (acc[64:65] + jnp.exp(sink - m))


def _window(k, qt, vt, sink):
    b, g, dk, s = qt.shape
    tq = min(TQ, s)
    win = tq + 2 * B_WINDOW
    return pl.pallas_call(
        functools.partial(_window_kernel, s_len=s, win=win),
        grid=(b, s // tq),
        in_specs=[
            pl.BlockSpec(memory_space=pltpu.SMEM),
            pl.BlockSpec((1, 1, s, dk), lambda bi, i: (bi, 0, 0, 0)),
            pl.BlockSpec((1, g, dk, tq), lambda bi, i: (bi, 0, 0, i)),
            pl.BlockSpec((1, B_KV_HEADS, V_ROWS, s), lambda bi, i: (bi, 0, 0, 0)),
        ],
        out_specs=pl.BlockSpec((1, g, 64, tq), lambda bi, i: (bi, 0, 0, i)),
        out_shape=jax.ShapeDtypeStruct((b, g, 64, s), F32),
        compiler_params=pltpu.CompilerParams(
            dimension_semantics=("arbitrary", "arbitrary"), vmem_limit_bytes=VMEM_LIMIT),
        name="window",
    )(sink.astype(F32), k, qt, vt)


def _nbr_kernel(bias_ref, k_ref, q_ref, v_ref, o_ref, s_buf, *, s_len, win):
    i = pl.program_id(1)
    tq = q_ref.shape[3]
    start = pl.multiple_of(jnp.clip(i * tq - (C_MAX_ROWS // 2) * GRID_W, 0, s_len - win), 256)
    for h in range(C_HEADS):
        k = k_ref[0, 0, pl.ds(start, win), 128 * (h // 2):128 * (h // 2) + 128]
        s_buf[h] = _dot(k, q_ref[0, h])
    for h in range(C_HEADS):
        s = s_buf[h] + bias_ref[0, h]
        m = jnp.max(s, axis=0, keepdims=True)
        p = jnp.exp(s - m).astype(BF16)
        acc = _dot(v_ref[0, h, :, pl.ds(start, win)], p)
        o_ref[0, h] = acc[0:64] / acc[64:65]


def _nbr_bias(rpb, rows):
    col = np.arange(GRID_W)
    cs = np.clip(col - C_WIN_COLS // 2, 0, GRID_W - C_WIN_COLS)
    col_ok = (col[:, None] >= cs[None, :]) & (col[:, None] < cs[None, :] + C_WIN_COLS)
    dc = np.clip(col[:, None] - col[None, :], -(C_WIN_COLS - 1), C_WIN_COLS - 1) + (C_WIN_COLS - 1)
    rp = rpb.astype(F32)
    by_cols = jnp.zeros(rp.shape[:2] + dc.shape, F32)
    for b in range(rp.shape[2]):
        by_cols = by_cols + jnp.where((dc == b)[None, None], rp[:, :, b][:, :, None, None], 0.0)
    kr = min(C_MAX_ROWS, rows)
    out = []
    for r0 in (0, C_QROWS, rows - C_QROWS):
        rk0 = min(max(r0 - C_MAX_ROWS // 2, 0), rows - C_KROWS)
        rq = r0 + np.arange(C_QROWS)
        rk = rk0 + np.arange(C_KROWS)
        rs = np.clip(rq - kr // 2, 0, rows - kr)
        row_ok = (rk[:, None] >= rs[None, :]) & (rk[:, None] < rs[None, :] + kr)
        dr = np.clip(rk[:, None] - rq[None, :] + (C_MAX_ROWS - 1), 0, 2 * C_MAX_ROWS - 2)
        blocks = jnp.stack([jnp.stack([by_cols[:, dr[a, c]] for c in range(C_QROWS)], axis=2)
                            for a in range(C_KROWS)], axis=1)
        ok = row_ok[:, None, :, None] & col_ok[None, :, None, :]
        bias = jnp.where(ok[None], blocks, NEG_INF)
        out.append(bias.reshape(rp.shape[0], C_KROWS * GRID_W, C_QROWS * GRID_W))
    return jnp.stack(out)


def _nbr(k, qt, vt, rpb):
    b, g, dk, s = qt.shape
    tq = C_QROWS * GRID_W
    win = C_KROWS * GRID_W
    nq = s // tq
    bias = _nbr_bias(rpb, s // GRID_W)

    def variant(i):
        return jnp.where(i == 0, 0, jnp.where(i == nq - 1, 2, 1))

    return pl.pallas_call(
        functools.partial(_nbr_kernel, s_len=s, win=win),
        grid=(b, nq),
        scratch_shapes=[pltpu.VMEM((g, win, tq), F32)],
        in_specs=[
            pl.BlockSpec((1, g, win, tq), lambda bi, i: (variant(i), 0, 0, 0)),
            pl.BlockSpec((1, 1, s, k.shape[3]), lambda bi, i: (bi, 0, 0, 0)),
            pl.BlockSpec((1, g, dk, tq), lambda bi, i: (bi, 0, 0, i)),
            pl.BlockSpec((1, g, V_ROWS, s), lambda bi, i: (bi, 0, 0, 0)),
        ],
        out_specs=pl.BlockSpec((1, g, 64, tq), lambda bi, i: (bi, 0, 0, i)),
        out_shape=jax.ShapeDtypeStruct((b, g, 64, s), F32),
        compiler_params=pltpu.CompilerParams(
            dimension_semantics=("arbitrary", "arbitrary"), vmem_limit_bytes=VMEM_LIMIT),
        name="nbr",
    )(bias, k, qt, vt)


def _out_kernel(oa_ref, ob_ref, oc_ref, od_ref, sg_ref, h_ref, p_ref, lam_ref, sub_ref,
                wo_ref, pg_ref, wg_ref, wp_ref, fg_ref, y_ref, *, lam_init, final):
    lv = lam_ref[...]
    lam = (jnp.exp(jnp.sum(lv[0:1] * lv[1:2], axis=-1, keepdims=True))
           - jnp.exp(jnp.sum(lv[2:3] * lv[3:4], axis=-1, keepdims=True)) + lam_init)
    parts = []
    for h in range(A_HEADS):
        o = oa_ref[0, 2 * h] - lam * oa_ref[0, 2 * h + 1]
        on = (o * lax.rsqrt(jnp.mean(o * o, axis=0, keepdims=True) + EPS)) * sub_ref[...]
        parts.append(on * (1.0 - lam_init))
    for ref in (ob_ref, oc_ref, od_ref):
        for h in range(4):
            parts.append(ref[0, h])
    mix = (jnp.concatenate(parts, axis=0) * sg_ref[0]).astype(BF16)
    h1 = h_ref[0] + _tn(mix, wo_ref[...])
    hn = ((h1 * lax.rsqrt(jnp.mean(h1 * h1, axis=-1, keepdims=True) + EPS)) * pg_ref[...]).astype(BF16)
    gate = jax.nn.sigmoid(_dot(hn, wg_ref[...]))
    h2 = h1 + gate * _dot(p_ref[0].astype(BF16), wp_ref[...])
    if final:
        h2 = (h2 * lax.rsqrt(jnp.mean(h2 * h2, axis=-1, keepdims=True) + EPS)) * fg_ref[...]
    y_ref[0] = h2


def _out(oa, ob, oc, od, sg, h, p, a_lambda, subln_g, w_out, ple_norm_g, w_ple_gate, w_ple_proj,
         final_norm_g, lam_init, final):
    b, s, _ = h.shape
    tm = min(TM, s)

    def heads(n):
        return pl.BlockSpec((1, n, 64, tm), lambda bi, i: (bi, 0, 0, i))

    return pl.pallas_call(
        functools.partial(_out_kernel, lam_init=lam_init, final=final),
        grid=(b, s // tm),
        in_specs=[
            heads(8), heads(4), heads(4), heads(4),
            pl.BlockSpec((1, D_MIX, tm), lambda bi, i: (bi, 0, i)),
            pl.BlockSpec((1, tm, D_MODEL), lambda bi, i: (bi, i, 0)),
            pl.BlockSpec((1, tm, D_PLE), lambda bi, i: (bi, i, 0)),
            _const_spec((4, A_QK_DIM)), _const_spec((A_V_DIM, 1)),
            _const_spec((D_MIX, D_MODEL)), _const_spec((1, D_MODEL)),
            _const_spec((D_MODEL, D_MODEL)), _const_spec((D_PLE, D_MODEL)), _const_spec((1, D_MODEL)),
        ],
        out_specs=pl.BlockSpec((1, tm, D_MODEL), lambda bi, i: (bi, i, 0)),
        out_shape=jax.ShapeDtypeStruct((b, s, D_MODEL), F32),
        compiler_params=pltpu.CompilerParams(
            dimension_semantics=("arbitrary", "arbitrary"), vmem_limit_bytes=VMEM_LIMIT),
        name="out",
    )(oa, ob, oc, od, sg, h, p, a_lambda.astype(F32), subln_g.reshape(A_V_DIM, 1).astype(F32),
      w_out.astype(BF16), ple_norm_g.reshape(1, D_MODEL), w_ple_gate.astype(BF16), w_ple_proj.astype(BF16),
      final_norm_g.reshape(1, D_MODEL))


def _trunk(x, p, norm_g, w_in, a_lambda, a_subln_g, b_sink, c_rpb, d_q_norm_g, d_kv_norm_g,
           d_w_uq, d_w_ukv, w_out, ple_norm_g, w_ple_gate, w_ple_proj, final_norm_g):
    depth = w_in.shape[0]
    s = x.shape[1]
    tabs = _rope_tables(s)
    h = x
    for i in range(depth):
        prep = _prep_layer(w_in[i], d_w_uq[i], d_w_ukv[i])
        (qa, ka, va, qb, kb, vb, qc, kc, vc, qd, kd, vd, sg) = _proj(
            h, norm_g[i], prep, tabs, d_q_norm_g[i], d_kv_norm_g[i])
        oa = _flash(ka, qa, va, lambda g: (0, 0, g // 4), lambda g: g // 2)
        od = _flash(kd, qd, vd, lambda g: (0, 0, g), lambda g: g)
        ob = _window(kb, qb, vb, b_sink[i])
        oc = _nbr(kc, qc, vc, c_rpb[i])
        lam_init = 0.8 - 0.6 * math.exp(-0.3 * i)
        h = _out(oa, ob, oc, od, sg, h, p[i], a_lambda[i], a_subln_g[i], w_out[i], ple_norm_g[i],
                 w_ple_gate[i], w_ple_proj[i], final_norm_g, lam_init, i == depth - 1)
    return h


def kernel(x_prompt, x_sample, p_prompt, p_sample, norm_g, w_in, a_lambda, a_subln_g, b_sink, c_rpb,
           d_q_norm_g, d_kv_norm_g, d_w_uq, d_w_ukv, w_out, ple_norm_g, w_ple_gate, w_ple_proj,
           final_norm_g):
    weights = (norm_g, w_in, a_lambda, a_subln_g, b_sink, c_rpb, d_q_norm_g, d_kv_norm_g, d_w_uq, d_w_ukv,
               w_out, ple_norm_g, w_ple_gate, w_ple_proj, final_norm_g)
    return (_trunk(x_prompt, p_prompt, *weights), _trunk(x_sample, p_sample, *weights))
```

```python
import functools
import math

import jax
import jax.numpy as jnp
import numpy as np
from jax import lax
from jax.experimental import pallas as pl
from jax.experimental.pallas import tpu as pltpu

F32 = jnp.float32
BF16 = jnp.bfloat16

D_MODEL = 1024
D_MIX = 1024
D_PLE = 256
GRID_W = 64
ROPE_THETA = 10000.0
EPS = 1e-6
NEG_INF = -1e30

A_HEADS, A_QK_DIM, A_V_DIM = 4, 32, 64
B_HEADS, B_KV_HEADS, B_HEAD_DIM, B_WINDOW = 4, 2, 64, 128
C_HEADS, C_HEAD_DIM, C_MAX_ROWS, C_WIN_COLS = 4, 64, 8, 16
D_HEADS, D_Q_RANK, D_KV_RANK, D_NOPE, D_ROPE, D_V = 4, 256, 128, 64, 32, 64

IN_SIZES = (256, 256, 256, 256, 128, 128, 256, 256, 256, D_Q_RANK, D_KV_RANK, D_ROPE, D_MIX)

V_ROWS = 80
VMEM_LIMIT = 56 * 1024 * 1024

TM = 512
TQ = 256
TQ_DENSE = 512
LOG2E = math.log2(math.e)
D_SCALE = (D_NOPE + D_ROPE) ** -0.5 * LOG2E
TK = 256
FLASH_BUF_BYTES = 24 * 1024 * 1024
C_QROWS = 4
C_KROWS = 12


def _nt(a, b):
    return lax.dot_general(a, b, (((1,), (1,)), ((), ())), preferred_element_type=F32)


def _tn(a, b):
    return lax.dot_general(a, b, (((0,), (0,)), ((), ())), preferred_element_type=F32)


def _dot(a, b):
    return jnp.dot(a, b, preferred_element_type=F32)


T_AQ, T_AQR, T_BQ, T_BQR, T_CQ, T_DCQ, T_DCKV, T_AV, T_BV, T_CV, T_GATE, T_END = (
    0, 256, 512, 768, 1024, 1280, 1536, 1664, 1920, 2048, 2304, 3328)
S_AK, S_AKR, S_BK, S_BKR, S_CK, S_DCKV, S_KPE, S_KPER, S_END = (
    0, 256, 512, 640, 768, 1024, 1152, 1280, 1408)


def _proj_kernel(h_ref, g_ref, wt_ref, ws_ref, ta_ref, tb_ref, td_ref, sa_ref, sb_ref, sd_ref,
                 qg_ref, kvgc_ref, kvgr_ref, wqn_ref, wqp_ref, wqpr_ref, wkn_ref, place_ref, wvt_ref,
                 qa_ref, ka_ref, va_ref, qb_ref, kb_ref, vb_ref, qc_ref, kc_ref, vc_ref,
                 qd_ref, kd_ref, vd_ref, sg_ref):
    x = h_ref[0]
    tm = x.shape[0]
    ms = jnp.mean(x * x, axis=-1, keepdims=True)
    hn = ((x * lax.rsqrt(ms + EPS)) * g_ref[...]).astype(BF16)

    def zt(r0, r1):
        return _nt(wt_ref[r0:r1, :], hn)

    ones_rows = (lax.broadcasted_iota(jnp.int32, (V_ROWS - 64, tm), 0) == 0).astype(BF16)

    def store_values(ref, v, heads):
        for h in range(heads):
            ref[0, h, 0:64, :] = v[64 * h:64 * h + 64].astype(BF16)
            ref[0, h, 64:V_ROWS, :] = ones_rows

    qa = zt(T_AQ, T_AQR) * ta_ref[0] + zt(T_AQR, T_BQ) * ta_ref[1]
    for g in range(2 * A_HEADS):
        r = 32 * (g % 4)
        qa_ref[0, g] = jnp.zeros((128, tm), BF16)
        qa_ref[0, g, r:r + 32, :] = qa[32 * g:32 * g + 32].astype(BF16)
    store_values(va_ref, zt(T_AV, T_BV), A_HEADS)

    qb = zt(T_BQ, T_BQR) * tb_ref[0] + zt(T_BQR, T_CQ) * tb_ref[1]
    for h in range(B_HEADS):
        r = 64 * (h // 2)
        qb_ref[0, h] = jnp.zeros((128, tm), BF16)
        qb_ref[0, h, r:r + 64, :] = qb[64 * h:64 * h + 64].astype(BF16)
    store_values(vb_ref, zt(T_BV, T_CV), B_KV_HEADS)

    qc = zt(T_CQ, T_DCQ) * (C_HEAD_DIM ** -0.5)
    for h in range(C_HEADS):
        r = 64 * (h % 2)
        qc_ref[0, h] = jnp.zeros((128, tm), BF16)
        qc_ref[0, h, r:r + 64, :] = qc[64 * h:64 * h + 64].astype(BF16)
    store_values(vc_ref, zt(T_CV, T_GATE), C_HEADS)

    cq = zt(T_DCQ, T_DCKV)
    cqn = ((cq * lax.rsqrt(jnp.mean(cq * cq, axis=0, keepdims=True) + EPS)) * qg_ref[...]).astype(BF16)
    for h in range(D_HEADS):
        qn = _dot(wqn_ref[h], cqn) * D_SCALE
        qpe = _dot(wqp_ref[h], cqn) * td_ref[0] + _dot(wqpr_ref[h], cqn) * td_ref[1]
        qd_ref[0, h, 0:64, :] = qn.astype(BF16)
        qd_ref[0, h, 64:96, :] = qpe.astype(BF16)
        qd_ref[0, h, 96:128, :] = jnp.zeros((32, tm), BF16)
    ckt = zt(T_DCKV, T_AV)
    cktn = ((ckt * lax.rsqrt(jnp.mean(ckt * ckt, axis=0, keepdims=True) + EPS)) * kvgc_ref[...]).astype(BF16)
    store_values(vd_ref, _dot(wvt_ref[...], cktn), D_HEADS)

    gt = zt(T_GATE, T_END)
    sg_ref[0] = gt * jax.nn.sigmoid(gt)

    zs = _dot(hn, ws_ref[...])
    ka_ref[0, 0] = (zs[:, S_AK:S_AKR] * sa_ref[0] + zs[:, S_AKR:S_BK] * sa_ref[1]).astype(BF16)
    kb_ref[0, 0] = (zs[:, S_BK:S_BKR] * sb_ref[0] + zs[:, S_BKR:S_CK] * sb_ref[1]).astype(BF16)
    kc_ref[0, 0] = zs[:, S_CK:S_DCKV].astype(BF16)
    ck = zs[:, S_DCKV:S_KPE]
    ckn = ((ck * lax.rsqrt(jnp.mean(ck * ck, axis=-1, keepdims=True) + EPS)) * kvgr_ref[...]).astype(BF16)
    kpe = (zs[:, S_KPE:S_KPER] * sd_ref[0] + zs[:, S_KPER:S_END] * sd_ref[1]).astype(BF16)
    kd_ref[0, 0] = (_dot(ckn, wkn_ref[...]) + _dot(kpe, place_ref[...])).astype(BF16)


def _rope_cs(s, dim):
    inv = 1.0 / (ROPE_THETA ** (jnp.arange(0, dim, 2, dtype=F32) / dim))
    ang = jnp.arange(s, dtype=F32)[:, None] * inv[None, :]
    cos, sin = jnp.cos(ang), jnp.sin(ang)
    return jnp.concatenate([cos, cos], -1), jnp.concatenate([-sin, sin], -1)


def _rope_tables(s):
    c32, s32 = _rope_cs(s, A_QK_DIM)
    c64, s64 = _rope_cs(s, B_HEAD_DIM)
    a_scale, b_scale, d_scale = A_QK_DIM ** -0.5 * LOG2E, B_HEAD_DIM ** -0.5, D_SCALE
    ta = jnp.stack([jnp.tile(c32, (1, 8)).T, jnp.tile(s32, (1, 8)).T]) * a_scale
    tb = jnp.stack([jnp.tile(c64, (1, 4)).T, jnp.tile(s64, (1, 4)).T]) * b_scale
    td = jnp.stack([c32.T, s32.T]) * d_scale
    sa = jnp.stack([jnp.tile(c32, (1, 8)), jnp.tile(s32, (1, 8))])
    sb = jnp.stack([jnp.tile(c64, (1, 2)), jnp.tile(s64, (1, 2))])
    pad = jnp.zeros((s, 128 - D_ROPE), F32)
    sd = jnp.stack([jnp.concatenate([c32, pad], -1), jnp.concatenate([s32, pad], -1)])
    return ta, tb, td, sa, sb, sd


def _rot_cols(w, d):
    k, n = w.shape
    return jnp.roll(w.reshape(k, n // d, d), d // 2, axis=-1).reshape(k, n)


def _prep_layer(w_in, d_w_uq, d_w_ukv):
    pts = [0]
    for n in IN_SIZES:
        pts.append(pts[-1] + n)
    (aq, ak, av, bq, bk, bv, cq, ck, cv, dcq, dckv, dkr, gate) = [w_in[:, pts[i]:pts[i + 1]] for i in range(13)]
    wt = jnp.concatenate([aq, _rot_cols(aq, A_QK_DIM), bq, _rot_cols(bq, B_HEAD_DIM), cq, dcq, dckv,
                          av, bv, cv, gate], axis=1).T.astype(BF16)
    zpad = jnp.zeros((D_MODEL, 128 - D_ROPE), F32)
    ws = jnp.concatenate([ak, _rot_cols(ak, A_QK_DIM), bk, _rot_cols(bk, B_HEAD_DIM), ck, dckv,
                          dkr, zpad, _rot_cols(dkr, D_ROPE), zpad], axis=1).astype(BF16)
    uq = d_w_uq.reshape(D_Q_RANK, D_HEADS, D_NOPE + D_ROPE)
    uq_n, uq_p = uq[:, :, :D_NOPE], uq[:, :, D_NOPE:]
    wqn = jnp.transpose(uq_n, (1, 2, 0))
    wqp = jnp.transpose(uq_p, (1, 2, 0))
    wqpr = jnp.transpose(jnp.roll(uq_p, D_ROPE // 2, axis=-1), (1, 2, 0))
    ukv = d_w_ukv.reshape(D_KV_RANK, D_HEADS, D_NOPE + D_V)
    wkn = jnp.pad(ukv[:, :, :D_NOPE], ((0, 0), (0, 0), (0, 128 - D_NOPE))).reshape(D_KV_RANK, D_HEADS * 128)
    place = np.zeros((128, D_HEADS, 128), np.float32)
    place[np.arange(D_ROPE), :, D_NOPE + np.arange(D_ROPE)] = 1.0
    place = jnp.asarray(place.reshape(128, D_HEADS * 128), BF16)
    wvt = ukv[:, :, D_NOPE:].reshape(D_KV_RANK, D_HEADS * D_V).T
    return (wt, ws, wqn.astype(BF16), wqp.astype(BF16), wqpr.astype(BF16), wkn.astype(BF16), place,
            wvt.astype(BF16))


def _const_spec(shape):
    nd = len(shape)
    return pl.BlockSpec(shape, lambda *_: (0,) * nd)


def _proj(h, norm_g, prep, tabs, q_norm_g, kv_norm_g):
    b, s, _ = h.shape
    tm = min(TM, s)
    wt, ws, wqn, wqp, wqpr, wkn, place, wvt = prep
    ta, tb, td, sa, sb, sd = tabs

    def tok(shape_tail):
        return pl.BlockSpec((1, 1, tm) + shape_tail, lambda i, j: (j, 0, i, 0))

    def chan(groups, rows):
        return pl.BlockSpec((1, groups, rows, tm), lambda i, j: (j, 0, 0, i))

    in_specs = [
        pl.BlockSpec((1, tm, D_MODEL), lambda i, j: (j, i, 0)),
        _const_spec((1, D_MODEL)), _const_spec(wt.shape), _const_spec(ws.shape),
        pl.BlockSpec((2, 256, tm), lambda i, j: (0, 0, i)),
        pl.BlockSpec((2, 256, tm), lambda i, j: (0, 0, i)),
        pl.BlockSpec((2, D_ROPE, tm), lambda i, j: (0, 0, i)),
        pl.BlockSpec((2, tm, 256), lambda i, j: (0, i, 0)),
        pl.BlockSpec((2, tm, 128), lambda i, j: (0, i, 0)),
        pl.BlockSpec((2, tm, 128), lambda i, j: (0, i, 0)),
        _const_spec((D_Q_RANK, 1)), _const_spec((D_KV_RANK, 1)), _const_spec((1, D_KV_RANK)),
        _const_spec(wqn.shape), _const_spec(wqp.shape), _const_spec(wqpr.shape),
        _const_spec(wkn.shape), _const_spec(place.shape), _const_spec(wvt.shape),
    ]
    out_shape = [
        jax.ShapeDtypeStruct((b, 8, 128, s), BF16),
        jax.ShapeDtypeStruct((b, 1, s, 256), BF16),
        jax.ShapeDtypeStruct((b, A_HEADS, V_ROWS, s), BF16),
        jax.ShapeDtypeStruct((b, B_HEADS, 128, s), BF16),
        jax.ShapeDtypeStruct((b, 1, s, 128), BF16),
        jax.ShapeDtypeStruct((b, B_KV_HEADS, V_ROWS, s), BF16),
        jax.ShapeDtypeStruct((b, C_HEADS, 128, s), BF16),
        jax.ShapeDtypeStruct((b, 1, s, 256), BF16),
        jax.ShapeDtypeStruct((b, C_HEADS, V_ROWS, s), BF16),
        jax.ShapeDtypeStruct((b, D_HEADS, 128, s), BF16),
        jax.ShapeDtypeStruct((b, 1, s, D_HEADS * 128), BF16),
        jax.ShapeDtypeStruct((b, D_HEADS, V_ROWS, s), BF16),
        jax.ShapeDtypeStruct((b, D_MIX, s), F32),
    ]
    out_specs = [
        chan(8, 128), tok((256,)), chan(A_HEADS, V_ROWS),
        chan(B_HEADS, 128), tok((128,)), chan(B_KV_HEADS, V_ROWS),
        chan(C_HEADS, 128), tok((256,)), chan(C_HEADS, V_ROWS),
        chan(D_HEADS, 128), tok((D_HEADS * 128,)), chan(D_HEADS, V_ROWS),
        pl.BlockSpec((1, D_MIX, tm), lambda i, j: (j, 0, i)),
    ]
    return pl.pallas_call(
        _proj_kernel,
        grid=(s // tm, b),
        in_specs=in_specs, out_specs=out_specs, out_shape=out_shape,
        compiler_params=pltpu.CompilerParams(
            dimension_semantics=("arbitrary", "arbitrary"), vmem_limit_bytes=VMEM_LIMIT),
        name="proj",
    )(h, norm_g.reshape(1, D_MODEL), wt, ws, ta, tb, td, sa, sb, sd,
      q_norm_g.reshape(D_Q_RANK, 1), kv_norm_g.reshape(D_KV_RANK, 1), kv_norm_g.reshape(1, D_KV_RANK),
      wqn, wqp, wqpr, wkn, place, wvt)


def _flash_kernel(k_ref, q_ref, v_ref, o_ref, s_buf, p_buf, pv_buf, *, tk, nk):
    q = q_ref[0, 0]
    tq = q.shape[1]

    def scores(j):
        s = _dot(k_ref[0, 0, j * tk:(j + 1) * tk, :], q)
        s_buf[j] = s
        return jnp.max(s, axis=0, keepdims=True)

    def softmax(j, m, cmax):
        m_new = jnp.maximum(m, cmax)
        p_buf[j] = jnp.exp2(s_buf[j] - m_new).astype(BF16)
        return m_new, jnp.exp2(m - m_new)

    def values(j, acc, alpha):
        pv_buf[...] = _dot(v_ref[0, 0, :, j * tk:(j + 1) * tk], p_buf[j])
        return acc * alpha + pv_buf[...]

    m = jnp.full((1, tq), NEG_INF, F32)
    acc = jnp.zeros((V_ROWS, tq), F32)
    cmax = {0: scores(0)}
    alpha = {}
    for j in range(nk + 1):
        if j + 1 < nk:
            cmax[j + 1] = scores(j + 1)
        if j < nk:
            m, alpha[j] = softmax(j, m, cmax.pop(j))
        if j >= 1:
            acc = values(j - 1, acc, alpha.pop(j - 1))
    o_ref[0, 0] = acc[0:64] / acc[64:65]


def _flash(k, qt, vt, k_group, v_group):
    b, g, dk, s = qt.shape
    tq, tk = min(TQ_DENSE, s), min(TK, s)
    nk = s // tk
    slots = nk
    assert slots * tk * tq * 6 <= FLASH_BUF_BYTES
    return pl.pallas_call(
        functools.partial(_flash_kernel, tk=tk, nk=nk),
        grid=(b, g, s // tq),
        scratch_shapes=[pltpu.VMEM((slots, tk, tq), F32), pltpu.VMEM((slots, tk, tq), BF16),
                        pltpu.VMEM((V_ROWS, tq), F32)],
        in_specs=[
            pl.BlockSpec((1, 1, s, dk), lambda bi, gi, i: (bi,) + k_group(gi)),
            pl.BlockSpec((1, 1, dk, tq), lambda bi, gi, i: (bi, gi, 0, i)),
            pl.BlockSpec((1, 1, V_ROWS, s), lambda bi, gi, i: (bi, v_group(gi), 0, 0)),
        ],
        out_specs=pl.BlockSpec((1, 1, 64, tq), lambda bi, gi, i: (bi, gi, 0, i)),
        out_shape=jax.ShapeDtypeStruct((b, g, 64, s), F32),
        compiler_params=pltpu.CompilerParams(
            dimension_semantics=("arbitrary", "arbitrary", "arbitrary"), vmem_limit_bytes=VMEM_LIMIT),
        name="flash",
    )(k, qt, vt)


def _window_kernel(sink_ref, k_ref, q_ref, v_ref, o_ref, *, s_len, win):
    i = pl.program_id(1)
    tq = q_ref.shape[3]
    q0 = i * tq
    start = pl.multiple_of(jnp.clip(q0 - B_WINDOW, 0, s_len - win), 128)
    k = k_ref[0, 0, pl.ds(start, win), :]
    kpos = start + lax.broadcasted_iota(jnp.int32, (win, tq), 0)
    qpos = q0 + lax.broadcasted_iota(jnp.int32, (win, tq), 1)
    in_band = jnp.abs(kpos - qpos) <= B_WINDOW
    scores = [_dot(k, q_ref[0, h]) for h in range(B_HEADS)]
    for h in range(B_HEADS):
        s = jnp.where(in_band, scores[h], NEG_INF)
        sink = sink_ref[h]
        m = jnp.maximum(jnp.max(s, axis=0, keepdims=True), sink)
        p = jnp.exp(s - m).astype(BF16)
        acc = _dot(v_ref[0, h // (B_HEADS // B_KV_HEADS), :, pl.ds(start, win)], p)
        o_ref[0, h] = acc[0:64] / (acc[64:65] + jnp.exp(sink - m))


def _window(k, qt, vt, sink):
    b, g, dk, s = qt.shape
    tq = min(TQ, s)
    win = tq + 2 * B_WINDOW
    return pl.pallas_call(
        functools.partial(_window_kernel, s_len=s, win=win),
        grid=(b, s // tq),
        in_specs=[
            pl.BlockSpec(memory_space=pltpu.SMEM),
            pl.BlockSpec((1, 1, s, dk), lambda bi, i: (bi, 0, 0, 0)),
            pl.BlockSpec((1, g, dk, tq), lambda bi, i: (bi, 0, 0, i)),
            pl.BlockSpec((1, B_KV_HEADS, V_ROWS, s), lambda bi, i: (bi, 0, 0, 0)),
        ],
        out_specs=pl.BlockSpec((1, g, 64, tq), lambda bi, i: (bi, 0, 0, i)),
        out_shape=jax.ShapeDtypeStruct((b, g, 64, s), F32),
        compiler_params=pltpu.CompilerParams(
            dimension_semantics=("arbitrary", "arbitrary"), vmem_limit_bytes=VMEM_LIMIT),
        name="window",
    )(sink.astype(F32), k, qt, vt)


def _nbr_kernel(bias_ref, k_ref, q_ref, v_ref, o_ref, s_buf, *, s_len, win):
    i = pl.program_id(1)
    tq = q_ref.shape[3]
    start = pl.multiple_of(jnp.clip(i * tq - (C_MAX_ROWS // 2) * GRID_W, 0, s_len - win), 256)
    for h in range(C_HEADS):
        k = k_ref[0, 0, pl.ds(start, win), 128 * (h // 2):128 * (h // 2) + 128]
        s_buf[h] = _dot(k, q_ref[0, h])
    for h in range(C_HEADS):
        s = s_buf[h] + bias_ref[0, h]
        m = jnp.max(s, axis=0, keepdims=True)
        p = jnp.exp(s - m).astype(BF16)
        acc = _dot(v_ref[0, h, :, pl.ds(start, win)], p)
        o_ref[0, h] = acc[0:64] / acc[64:65]


def _nbr_bias(rpb, rows):
    col = np.arange(GRID_W)
    cs = np.clip(col - C_WIN_COLS // 2, 0, GRID_W - C_WIN_COLS)
    col_ok = (col[:, None] >= cs[None, :]) & (col[:, None] < cs[None, :] + C_WIN_COLS)
    dc = np.clip(col[:, None] - col[None, :], -(C_WIN_COLS - 1), C_WIN_COLS - 1) + (C_WIN_COLS - 1)
    rp = rpb.astype(F32)
    by_cols = jnp.zeros(rp.shape[:2] + dc.shape, F32)
    for b in range(rp.shape[2]):
        by_cols = by_cols + jnp.where((dc == b)[None, None], rp[:, :, b][:, :, None, None], 0.0)
    kr = min(C_MAX_ROWS, rows)
    out = []
    for r0 in (0, C_QROWS, rows - C_QROWS):
        rk0 = min(max(r0 - C_MAX_ROWS // 2, 0), rows - C_KROWS)
        rq = r0 + np.arange(C_QROWS)
        rk = rk0 + np.arange(C_KROWS)
        rs = np.clip(rq - kr // 2, 0, rows - kr)
        row_ok = (rk[:, None] >= rs[None, :]) & (rk[:, None] < rs[None, :] + kr)
        dr = np.clip(rk[:, None] - rq[None, :] + (C_MAX_ROWS - 1), 0, 2 * C_MAX_ROWS - 2)
        blocks = jnp.stack([jnp.stack([by_cols[:, dr[a, c]] for c in range(C_QROWS)], axis=2)
                            for a in range(C_KROWS)], axis=1)
        ok = row_ok[:, None, :, None] & col_ok[None, :, None, :]
        bias = jnp.where(ok[None], blocks, NEG_INF)
        out.append(bias.reshape(rp.shape[0], C_KROWS * GRID_W, C_QROWS * GRID_W))
    return jnp.stack(out)


def _nbr(k, qt, vt, rpb):
    b, g, dk, s = qt.shape
    tq = C_QROWS * GRID_W
    win = C_KROWS * GRID_W
    nq = s // tq
    bias = _nbr_bias(rpb, s // GRID_W)

    def variant(i):
        return jnp.where(i == 0, 0, jnp.where(i == nq - 1, 2, 1))

    return pl.pallas_call(
        functools.partial(_nbr_kernel, s_len=s, win=win),
        grid=(b, nq),
        scratch_shapes=[pltpu.VMEM((g, win, tq), F32)],
        in_specs=[
            pl.BlockSpec((1, g, win, tq), lambda bi, i: (variant(i), 0, 0, 0)),
            pl.BlockSpec((1, 1, s, k.shape[3]), lambda bi, i: (bi, 0, 0, 0)),
            pl.BlockSpec((1, g, dk, tq), lambda bi, i: (bi, 0, 0, i)),
            pl.BlockSpec((1, g, V_ROWS, s), lambda bi, i: (bi, 0, 0, 0)),
        ],
        out_specs=pl.BlockSpec((1, g, 64, tq), lambda bi, i: (bi, 0, 0, i)),
        out_shape=jax.ShapeDtypeStruct((b, g, 64, s), F32),
        compiler_params=pltpu.CompilerParams(
            dimension_semantics=("arbitrary", "arbitrary"), vmem_limit_bytes=VMEM_LIMIT),
        name="nbr",
    )(bias, k, qt, vt)


def _out_kernel(oa_ref, ob_ref, oc_ref, od_ref, sg_ref, h_ref, p_ref, lam_ref, sub_ref,
                wo_ref, pg_ref, wg_ref, wp_ref, fg_ref, y_ref, *, lam_init, final):
    lv = lam_ref[...]
    lam = (jnp.exp(jnp.sum(lv[0:1] * lv[1:2], axis=-1, keepdims=True))
           - jnp.exp(jnp.sum(lv[2:3] * lv[3:4], axis=-1, keepdims=True)) + lam_init)
    parts = []
    for h in range(A_HEADS):
        o = oa_ref[0, 2 * h] - lam * oa_ref[0, 2 * h + 1]
        on = (o * lax.rsqrt(jnp.mean(o * o, axis=0, keepdims=True) + EPS)) * sub_ref[...]
        parts.append(on * (1.0 - lam_init))
    for ref in (ob_ref, oc_ref, od_ref):
        for h in range(4):
            parts.append(ref[0, h])
    mix = (jnp.concatenate(parts, axis=0) * sg_ref[0]).astype(BF16)
    h1 = h_ref[0] + _tn(mix, wo_ref[...])
    hn = ((h1 * lax.rsqrt(jnp.mean(h1 * h1, axis=-1, keepdims=True) + EPS)) * pg_ref[...]).astype(BF16)
    gate = jax.nn.sigmoid(_dot(hn, wg_ref[...]))
    h2 = h1 + gate * _dot(p_ref[0].astype(BF16), wp_ref[...])
    if final:
        h2 = (h2 * lax.rsqrt(jnp.mean(h2 * h2, axis=-1, keepdims=True) + EPS)) * fg_ref[...]
    y_ref[0] = h2


def _out(oa, ob, oc, od, sg, h, p, a_lambda, subln_g, w_out, ple_norm_g, w_ple_gate, w_ple_proj,
         final_norm_g, lam_init, final):
    b, s, _ = h.shape
    tm = min(TM, s)

    def heads(n):
        return pl.BlockSpec((1, n, 64, tm), lambda bi, i: (bi, 0, 0, i))

    return pl.pallas_call(
        functools.partial(_out_kernel, lam_init=lam_init, final=final),
        grid=(b, s // tm),
        in_specs=[
            heads(8), heads(4), heads(4), heads(4),
            pl.BlockSpec((1, D_MIX, tm), lambda bi, i: (bi, 0, i)),
            pl.BlockSpec((1, tm, D_MODEL), lambda bi, i: (bi, i, 0)),
            pl.BlockSpec((1, tm, D_PLE), lambda bi, i: (bi, i, 0)),
            _const_spec((4, A_QK_DIM)), _const_spec((A_V_DIM, 1)),
            _const_spec((D_MIX, D_MODEL)), _const_spec((1, D_MODEL)),
            _const_spec((D_MODEL, D_MODEL)), _const_spec((D_PLE, D_MODEL)), _const_spec((1, D_MODEL)),
        ],
        out_specs=pl.BlockSpec((1, tm, D_MODEL), lambda bi, i: (bi, i, 0)),
        out_shape=jax.ShapeDtypeStruct((b, s, D_MODEL), F32),
        compiler_params=pltpu.CompilerParams(
            dimension_semantics=("arbitrary", "arbitrary"), vmem_limit_bytes=VMEM_LIMIT),
        name="out",
    )(oa, ob, oc, od, sg, h, p, a_lambda.astype(F32), subln_g.reshape(A_V_DIM, 1).astype(F32),
      w_out.astype(BF16), ple_norm_g.reshape(1, D_MODEL), w_ple_gate.astype(BF16), w_ple_proj.astype(BF16),
      final_norm_g.reshape(1, D_MODEL))


def _trunk(x, p, norm_g, w_in, a_lambda, a_subln_g, b_sink, c_rpb, d_q_norm_g, d_kv_norm_g,
           d_w_uq, d_w_ukv, w_out, ple_norm_g, w_ple_gate, w_ple_proj, final_norm_g):
    depth = w_in.shape[0]
    s = x.shape[1]
    tabs = _rope_tables(s)
    h = x
    for i in range(depth):
        prep = _prep_layer(w_in[i], d_w_uq[i], d_w_ukv[i])
        (qa, ka, va, qb, kb, vb, qc, kc, vc, qd, kd, vd, sg) = _proj(
            h, norm_g[i], prep, tabs, d_q_norm_g[i], d_kv_norm_g[i])
        oa = _flash(ka, qa, va, lambda g: (0, 0, g // 4), lambda g: g // 2)
        od = _flash(kd, qd, vd, lambda g: (0, 0, g), lambda g: g)
        ob = _window(kb, qb, vb, b_sink[i])
        oc = _nbr(kc, qc, vc, c_rpb[i])
        lam_init = 0.8 - 0.6 * math.exp(-0.3 * i)
        h = _out(oa, ob, oc, od, sg, h, p[i], a_lambda[i], a_subln_g[i], w_out[i], ple_norm_g[i],
                 w_ple_gate[i], w_ple_proj[i], final_norm_g, lam_init, i == depth - 1)
    return h


def kernel(x_prompt, x_sample, p_prompt, p_sample, norm_g, w_in, a_lambda, a_subln_g, b_sink, c_rpb,
           d_q_norm_g, d_kv_norm_g, d_w_uq, d_w_ukv, w_out, ple_norm_g, w_ple_gate, w_ple_proj,
           final_norm_g):
    weights = (norm_g, w_in, a_lambda, a_subln_g, b_sink, c_rpb, d_q_norm_g, d_kv_norm_g, d_w_uq, d_w_ukv,
               w_out, ple_norm_g, w_ple_gate, w_ple_proj, final_norm_g)
    return (_trunk(x_prompt, p_prompt, *weights), _trunk(x_sample, p_sample, *weights))
```

```python
import functools
import math

import jax
import jax.numpy as jnp
import numpy as np
from jax import lax
from jax.experimental import pallas as pl
from jax.experimental.pallas import tpu as pltpu

F32 = jnp.float32
BF16 = jnp.bfloat16

D_MODEL = 1024
D_MIX = 1024
D_PLE = 256
GRID_W = 64
ROPE_THETA = 10000.0
EPS = 1e-6
NEG_INF = -1e30

A_HEADS, A_QK_DIM, A_V_DIM = 4, 32, 64
B_HEADS, B_KV_HEADS, B_HEAD_DIM, B_WINDOW = 4, 2, 64, 128
C_HEADS, C_HEAD_DIM, C_MAX_ROWS, C_WIN_COLS = 4, 64, 8, 16
D_HEADS, D_Q_RANK, D_KV_RANK, D_NOPE, D_ROPE, D_V = 4, 256, 128, 64, 32, 64

IN_SIZES = (256, 256, 256, 256, 128, 128, 256, 256, 256, D_Q_RANK, D_KV_RANK, D_ROPE, D_MIX)

V_ROWS = 80
VMEM_LIMIT = 56 * 1024 * 1024

TM = 512
TQ = 256
TQ_DENSE = 512
LOG2E = math.log2(math.e)
D_SCALE = (D_NOPE + D_ROPE) ** -0.5 * LOG2E
TK = 256
FLASH_BUF_BYTES = 24 * 1024 * 1024
C_QROWS = 4
C_KROWS = 12


def _nt(a, b):
    return lax.dot_general(a, b, (((1,), (1,)), ((), ())), preferred_element_type=F32)


def _tn(a, b):
    return lax.dot_general(a, b, (((0,), (0,)), ((), ())), preferred_element_type=F32)


def _dot(a, b):
    return jnp.dot(a, b, preferred_element_type=F32)


T_AQ, T_BQ, T_CQ, T_DCQ, T_DCKV, T_AV, T_BV, T_CV, T_GATE, T_END = (
    0, 256, 512, 768, 1024, 1152, 1408, 1536, 1792, 2816)
S_AK, S_BK, S_CK, S_DCKV, S_KPE, S_END = (0, 256, 384, 640, 768, 896)


def _swap_row_halves(x, d):
    h = d // 2
    return jnp.concatenate([x[r + o:r + o + h] for r in range(0, x.shape[0], d) for o in (h, 0)], axis=0)


def _swap_lane_halves(x, d):
    n = x.shape[1]
    lane = lax.broadcasted_iota(jnp.int32, x.shape, 1)
    return jnp.where(lane % d < d // 2, pltpu.roll(x, n - d // 2, 1), pltpu.roll(x, d // 2, 1))


def _proj_kernel(h_ref, g_ref, wt_ref, ws_ref, ta_ref, tb_ref, td_ref, sa_ref, sb_ref, sd_ref,
                 qg_ref, kvgc_ref, kvgr_ref, wqn_ref, wqp_ref, wkn_ref, place_ref, wvt_ref,
                 qa_ref, ka_ref, va_ref, qb_ref, kb_ref, vb_ref, qc_ref, kc_ref, vc_ref,
                 qd_ref, kd_ref, vd_ref, sg_ref):
    x = h_ref[0]
    tm = x.shape[0]
    ms = jnp.mean(x * x, axis=-1, keepdims=True)
    hn = ((x * lax.rsqrt(ms + EPS)) * g_ref[...]).astype(BF16)

    zs = _dot(hn, ws_ref[...])
    zak, zbk, zpe = zs[:, S_AK:S_BK], zs[:, S_BK:S_CK], zs[:, S_KPE:S_END]
    ka_ref[0, 0] = (zak * sa_ref[0] + _swap_lane_halves(zak, A_QK_DIM) * sa_ref[1]).astype(BF16)
    kb_ref[0, 0] = (zbk * sb_ref[0] + _swap_lane_halves(zbk, B_HEAD_DIM) * sb_ref[1]).astype(BF16)
    kc_ref[0, 0] = zs[:, S_CK:S_DCKV].astype(BF16)
    ck = zs[:, S_DCKV:S_KPE]
    ckn = ((ck * lax.rsqrt(jnp.mean(ck * ck, axis=-1, keepdims=True) + EPS)) * kvgr_ref[...]).astype(BF16)
    kpe = (zpe * sd_ref[0] + _swap_lane_halves(zpe, D_ROPE) * sd_ref[1]).astype(BF16)
    kd_ref[0, 0] = (_dot(ckn, wkn_ref[...]) + _dot(kpe, place_ref[...])).astype(BF16)

    zq = _nt(wt_ref[T_AQ:T_AV, :], hn)
    zv = _nt(wt_ref[T_AV:T_GATE, :], hn)
    gt = _nt(wt_ref[T_GATE:T_END, :], hn)
    sg_ref[0] = gt * jax.nn.sigmoid(gt)

    ones_rows = (lax.broadcasted_iota(jnp.int32, (V_ROWS - 64, tm), 0) == 0).astype(BF16)

    def store_values(ref, v, heads):
        for h in range(heads):
            ref[0, h, 0:64, :] = v[64 * h:64 * h + 64].astype(BF16)
            ref[0, h, 64:V_ROWS, :] = ones_rows

    za = zq[T_AQ:T_BQ]
    qa = za * ta_ref[0] + _swap_row_halves(za, A_QK_DIM) * ta_ref[1]
    for g in range(2 * A_HEADS):
        r = 32 * (g % 4)
        qa_ref[0, g] = jnp.zeros((128, tm), BF16)
        qa_ref[0, g, r:r + 32, :] = qa[32 * g:32 * g + 32].astype(BF16)
    store_values(va_ref, zv[0:T_BV - T_AV], A_HEADS)

    zb = zq[T_BQ:T_CQ]
    qb = zb * tb_ref[0] + _swap_row_halves(zb, B_HEAD_DIM) * tb_ref[1]
    for h in range(B_HEADS):
        r = 64 * (h // 2)
        qb_ref[0, h] = jnp.zeros((128, tm), BF16)
        qb_ref[0, h, r:r + 64, :] = qb[64 * h:64 * h + 64].astype(BF16)
    store_values(vb_ref, zv[T_BV - T_AV:T_CV - T_AV], B_KV_HEADS)

    qc = zq[T_CQ:T_DCQ] * (C_HEAD_DIM ** -0.5)
    for h in range(C_HEADS):
        r = 64 * (h % 2)
        qc_ref[0, h] = jnp.zeros((128, tm), BF16)
        qc_ref[0, h, r:r + 64, :] = qc[64 * h:64 * h + 64].astype(BF16)
    store_values(vc_ref, zv[T_CV - T_AV:T_GATE - T_AV], C_HEADS)

    cq = zq[T_DCQ:T_DCKV]
    cqn = ((cq * lax.rsqrt(jnp.mean(cq * cq, axis=0, keepdims=True) + EPS)) * qg_ref[...]).astype(BF16)
    for h in range(D_HEADS):
        qn = _dot(wqn_ref[h], cqn) * D_SCALE
        qp = _dot(wqp_ref[h], cqn)
        qpe = qp * td_ref[0] + _swap_row_halves(qp, D_ROPE) * td_ref[1]
        qd_ref[0, h, 0:64, :] = qn.astype(BF16)
        qd_ref[0, h, 64:96, :] = qpe.astype(BF16)
        qd_ref[0, h, 96:128, :] = jnp.zeros((32, tm), BF16)
    ckt = zq[T_DCKV:T_AV]
    cktn = ((ckt * lax.rsqrt(jnp.mean(ckt * ckt, axis=0, keepdims=True) + EPS)) * kvgc_ref[...]).astype(BF16)
    store_values(vd_ref, _dot(wvt_ref[...], cktn), D_HEADS)


def _rope_cs(s, dim):
    inv = 1.0 / (ROPE_THETA ** (jnp.arange(0, dim, 2, dtype=F32) / dim))
    ang = jnp.arange(s, dtype=F32)[:, None] * inv[None, :]
    cos, sin = jnp.cos(ang), jnp.sin(ang)
    return jnp.concatenate([cos, cos], -1), jnp.concatenate([-sin, sin], -1)


def _rope_tables(s):
    c32, s32 = _rope_cs(s, A_QK_DIM)
    c64, s64 = _rope_cs(s, B_HEAD_DIM)
    a_scale, b_scale, d_scale = A_QK_DIM ** -0.5 * LOG2E, B_HEAD_DIM ** -0.5, D_SCALE
    ta = jnp.stack([jnp.tile(c32, (1, 8)).T, jnp.tile(s32, (1, 8)).T]) * a_scale
    tb = jnp.stack([jnp.tile(c64, (1, 4)).T, jnp.tile(s64, (1, 4)).T]) * b_scale
    td = jnp.stack([c32.T, s32.T]) * d_scale
    sa = jnp.stack([jnp.tile(c32, (1, 8)), jnp.tile(s32, (1, 8))])
    sb = jnp.stack([jnp.tile(c64, (1, 2)), jnp.tile(s64, (1, 2))])
    pad = jnp.zeros((s, 128 - D_ROPE), F32)
    sd = jnp.stack([jnp.concatenate([c32, pad], -1), jnp.concatenate([s32, pad], -1)])
    return ta, tb, td, sa, sb, sd


def _prep_layer(w_in, d_w_uq, d_w_ukv):
    pts = [0]
    for n in IN_SIZES:
        pts.append(pts[-1] + n)
    (aq, ak, av, bq, bk, bv, cq, ck, cv, dcq, dckv, dkr, gate) = [w_in[:, pts[i]:pts[i + 1]] for i in range(13)]
    wt = jnp.concatenate([aq, bq, cq, dcq, dckv, av, bv, cv, gate], axis=1).T.astype(BF16)
    zpad = jnp.zeros((D_MODEL, 128 - D_ROPE), F32)
    ws = jnp.concatenate([ak, bk, ck, dckv, dkr, zpad], axis=1).astype(BF16)
    uq = d_w_uq.reshape(D_Q_RANK, D_HEADS, D_NOPE + D_ROPE)
    uq_n, uq_p = uq[:, :, :D_NOPE], uq[:, :, D_NOPE:]
    wqn = jnp.transpose(uq_n, (1, 2, 0))
    wqp = jnp.transpose(uq_p, (1, 2, 0))
    ukv = d_w_ukv.reshape(D_KV_RANK, D_HEADS, D_NOPE + D_V)
    wkn = jnp.pad(ukv[:, :, :D_NOPE], ((0, 0), (0, 0), (0, 128 - D_NOPE))).reshape(D_KV_RANK, D_HEADS * 128)
    place = np.zeros((128, D_HEADS, 128), np.float32)
    place[np.arange(D_ROPE), :, D_NOPE + np.arange(D_ROPE)] = 1.0
    place = jnp.asarray(place.reshape(128, D_HEADS * 128), BF16)
    wvt = ukv[:, :, D_NOPE:].reshape(D_KV_RANK, D_HEADS * D_V).T
    return (wt, ws, wqn.astype(BF16), wqp.astype(BF16), wkn.astype(BF16), place, wvt.astype(BF16))


def _const_spec(shape):
    nd = len(shape)
    return pl.BlockSpec(shape, lambda *_: (0,) * nd)


def _proj(h, norm_g, prep, tabs, q_norm_g, kv_norm_g):
    b, s, _ = h.shape
    tm = min(TM, s)
    wt, ws, wqn, wqp, wkn, place, wvt = prep
    ta, tb, td, sa, sb, sd = tabs

    def tok(shape_tail):
        return pl.BlockSpec((1, 1, tm) + shape_tail, lambda i, j: (j, 0, i, 0))

    def chan(groups, rows):
        return pl.BlockSpec((1, groups, rows, tm), lambda i, j: (j, 0, 0, i))

    in_specs = [
        pl.BlockSpec((1, tm, D_MODEL), lambda i, j: (j, i, 0)),
        _const_spec((1, D_MODEL)), _const_spec(wt.shape), _const_spec(ws.shape),
        pl.BlockSpec((2, 256, tm), lambda i, j: (0, 0, i)),
        pl.BlockSpec((2, 256, tm), lambda i, j: (0, 0, i)),
        pl.BlockSpec((2, D_ROPE, tm), lambda i, j: (0, 0, i)),
        pl.BlockSpec((2, tm, 256), lambda i, j: (0, i, 0)),
        pl.BlockSpec((2, tm, 128), lambda i, j: (0, i, 0)),
        pl.BlockSpec((2, tm, 128), lambda i, j: (0, i, 0)),
        _const_spec((D_Q_RANK, 1)), _const_spec((D_KV_RANK, 1)), _const_spec((1, D_KV_RANK)),
        _const_spec(wqn.shape), _const_spec(wqp.shape),
        _const_spec(wkn.shape), _const_spec(place.shape), _const_spec(wvt.shape),
    ]
    out_shape = [
        jax.ShapeDtypeStruct((b, 8, 128, s), BF16),
        jax.ShapeDtypeStruct((b, 1, s, 256), BF16),
        jax.ShapeDtypeStruct((b, A_HEADS, V_ROWS, s), BF16),
        jax.ShapeDtypeStruct((b, B_HEADS, 128, s), BF16),
        jax.ShapeDtypeStruct((b, 1, s, 128), BF16),
        jax.ShapeDtypeStruct((b, B_KV_HEADS, V_ROWS, s), BF16),
        jax.ShapeDtypeStruct((b, C_HEADS, 128, s), BF16),
        jax.ShapeDtypeStruct((b, 1, s, 256), BF16),
        jax.ShapeDtypeStruct((b, C_HEADS, V_ROWS, s), BF16),
        jax.ShapeDtypeStruct((b, D_HEADS, 128, s), BF16),
        jax.ShapeDtypeStruct((b, 1, s, D_HEADS * 128), BF16),
        jax.ShapeDtypeStruct((b, D_HEADS, V_ROWS, s), BF16),
        jax.ShapeDtypeStruct((b, D_MIX, s), F32),
    ]
    out_specs = [
        chan(8, 128), tok((256,)), chan(A_HEADS, V_ROWS),
        chan(B_HEADS, 128), tok((128,)), chan(B_KV_HEADS, V_ROWS),
        chan(C_HEADS, 128), tok((256,)), chan(C_HEADS, V_ROWS),
        chan(D_HEADS, 128), tok((D_HEADS * 128,)), chan(D_HEADS, V_ROWS),
        pl.BlockSpec((1, D_MIX, tm), lambda i, j: (j, 0, i)),
    ]
    return pl.pallas_call(
        _proj_kernel,
        grid=(s // tm, b),
        in_specs=in_specs, out_specs=out_specs, out_shape=out_shape,
        compiler_params=pltpu.CompilerParams(
            dimension_semantics=("arbitrary", "arbitrary"), vmem_limit_bytes=VMEM_LIMIT),
        name="proj",
    )(h, norm_g.reshape(1, D_MODEL), wt, ws, ta, tb, td, sa, sb, sd,
      q_norm_g.reshape(D_Q_RANK, 1), kv_norm_g.reshape(D_KV_RANK, 1), kv_norm_g.reshape(1, D_KV_RANK),
      wqn, wqp, wkn, place, wvt)


def _flash_kernel(k_ref, q_ref, v_ref, o_ref, s_buf, p_buf, pv_buf, *, tk, nk):
    q = q_ref[0, 0]
    tq = q.shape[1]

    def scores(j):
        s = _dot(k_ref[0, 0, j * tk:(j + 1) * tk, :], q)
        s_buf[j] = s
        return jnp.max(s, axis=0, keepdims=True)

    def softmax(j, m, cmax):
        m_new = jnp.maximum(m, cmax)
        p_buf[j] = jnp.exp2(s_buf[j] - m_new).astype(BF16)
        return m_new, jnp.exp2(m - m_new)

    def values(j, acc, alpha):
        pv_buf[...] = _dot(v_ref[0, 0, :, j * tk:(j + 1) * tk], p_buf[j])
        return acc * alpha + pv_buf[...]

    m = jnp.full((1, tq), NEG_INF, F32)
    acc = jnp.zeros((V_ROWS, tq), F32)
    cmax = {0: scores(0)}
    alpha = {}
    for j in range(nk + 1):
        if j + 1 < nk:
            cmax[j + 1] = scores(j + 1)
        if j < nk:
            m, alpha[j] = softmax(j, m, cmax.pop(j))
        if j >= 1:
            acc = values(j - 1, acc, alpha.pop(j - 1))
    o_ref[0, 0] = acc[0:64] / acc[64:65]


def _flash(k, qt, vt, k_group, v_group):
    b, g, dk, s = qt.shape
    tq, tk = min(TQ_DENSE, s), min(TK, s)
    nk = s // tk
    slots = nk
    assert slots * tk * tq * 6 <= FLASH_BUF_BYTES
    return pl.pallas_call(
        functools.partial(_flash_kernel, tk=tk, nk=nk),
        grid=(b, g, s // tq),
        scratch_shapes=[pltpu.VMEM((slots, tk, tq), F32), pltpu.VMEM((slots, tk, tq), BF16),
                        pltpu.VMEM((V_ROWS, tq), F32)],
        in_specs=[
            pl.BlockSpec((1, 1, s, dk), lambda bi, gi, i: (bi,) + k_group(gi)),
            pl.BlockSpec((1, 1, dk, tq), lambda bi, gi, i: (bi, gi, 0, i)),
            pl.BlockSpec((1, 1, V_ROWS, s), lambda bi, gi, i: (bi, v_group(gi), 0, 0)),
        ],
        out_specs=pl.BlockSpec((1, 1, 64, tq), lambda bi, gi, i: (bi, gi, 0, i)),
        out_shape=jax.ShapeDtypeStruct((b, g, 64, s), F32),
        compiler_params=pltpu.CompilerParams(
            dimension_semantics=("arbitrary", "arbitrary", "arbitrary"), vmem_limit_bytes=VMEM_LIMIT),
        name="flash",
    )(k, qt, vt)


def _window_kernel(sink_ref, k_ref, q_ref, v_ref, o_ref, *, s_len, win):
    i = pl.program_id(1)
    tq = q_ref.shape[3]
    q0 = i * tq
    start = pl.multiple_of(jnp.clip(q0 - B_WINDOW, 0, s_len - win), 128)
    k = k_ref[0, 0, pl.ds(start, win), :]
    kpos = start + lax.broadcasted_iota(jnp.int32, (win, tq), 0)
    qpos = q0 + lax.broadcasted_iota(jnp.int32, (win, tq), 1)
    in_band = jnp.abs(kpos - qpos) <= B_WINDOW
    scores = [_dot(k, q_ref[0, h]) for h in range(B_HEADS)]
    for h in range(B_HEADS):
        s = jnp.where(in_band, scores[h], NEG_INF)
        sink = sink_ref[h]
        m = jnp.maximum(jnp.max(s, axis=0, keepdims=True), sink)
        p = jnp.exp(s - m).astype(BF16)
        acc = _dot(v_ref[0, h // (B_HEADS // B_KV_HEADS), :, pl.ds(start, win)], p)
        o_ref[0, h] = acc[0:64] / (acc[64:65] + jnp.exp(sink - m))


def _window(k, qt, vt, sink):
    b, g, dk, s = qt.shape
    tq = min(TQ, s)
    win = tq + 2 * B_WINDOW
    return pl.pallas_call(
        functools.partial(_window_kernel, s_len=s, win=win),
        grid=(b, s // tq),
        in_specs=[
            pl.BlockSpec(memory_space=pltpu.SMEM),
            pl.BlockSpec((1, 1, s, dk), lambda bi, i: (bi, 0, 0, 0)),
            pl.BlockSpec((1, g, dk, tq), lambda bi, i: (bi, 0, 0, i)),
            pl.BlockSpec((1, B_KV_HEADS, V_ROWS, s), lambda bi, i: (bi, 0, 0, 0)),
        ],
        out_specs=pl.BlockSpec((1, g, 64, tq), lambda bi, i: (bi, 0, 0, i)),
        out_shape=jax.ShapeDtypeStruct((b, g, 64, s), F32),
        compiler_params=pltpu.CompilerParams(
            dimension_semantics=("arbitrary", "arbitrary"), vmem_limit_bytes=VMEM_LIMIT),
        name="window",
    )(sink.astype(F32), k, qt, vt)


def _nbr_kernel(bias_ref, k_ref, q_ref, v_ref, o_ref, s_buf, *, s_len, win):
    i = pl.program_id(1)
    tq = q_ref.shape[3]
    start = pl.multiple_of(jnp.clip(i * tq - (C_MAX_ROWS // 2) * GRID_W, 0, s_len - win), 256)
    for h in range(C_HEADS):
        k = k_ref[0, 0, pl.ds(start, win), 128 * (h // 2):128 * (h // 2) + 128]
        s_buf[h] = _dot(k, q_ref[0, h])
    for h in range(C_HEADS):
        s = s_buf[h] + bias_ref[0, h]
        m = jnp.max(s, axis=0, keepdims=True)
        p = jnp.exp(s - m).astype(BF16)
        acc = _dot(v_ref[0, h, :, pl.ds(start, win)], p)
        o_ref[0, h] = acc[0:64] / acc[64:65]


def _nbr_bias(rpb, rows):
    col = np.arange(GRID_W)
    cs = np.clip(col - C_WIN_COLS // 2, 0, GRID_W - C_WIN_COLS)
    col_ok = (col[:, None] >= cs[None, :]) & (col[:, None] < cs[None, :] + C_WIN_COLS)
    dc = np.clip(col[:, None] - col[None, :], -(C_WIN_COLS - 1), C_WIN_COLS - 1) + (C_WIN_COLS - 1)
    rp = rpb.astype(F32)
    by_cols = jnp.zeros(rp.shape[:2] + dc.shape, F32)
    for b in range(rp.shape[2]):
        by_cols = by_cols + jnp.where((dc == b)[None, None], rp[:, :, b][:, :, None, None], 0.0)
    kr = min(C_MAX_ROWS, rows)
    out = []
    for r0 in (0, C_QROWS, rows - C_QROWS):
        rk0 = min(max(r0 - C_MAX_ROWS // 2, 0), rows - C_KROWS)
        rq = r0 + np.arange(C_QROWS)
        rk = rk0 + np.arange(C_KROWS)
        rs = np.clip(rq - kr // 2, 0, rows - kr)
        row_ok = (rk[:, None] >= rs[None, :]) & (rk[:, None] < rs[None, :] + kr)
        dr = np.clip(rk[:, None] - rq[None, :] + (C_MAX_ROWS - 1), 0, 2 * C_MAX_ROWS - 2)
        blocks = jnp.stack([jnp.stack([by_cols[:, dr[a, c]] for c in range(C_QROWS)], axis=2)
                            for a in range(C_KROWS)], axis=1)
        ok = row_ok[:, None, :, None] & col_ok[None, :, None, :]
        bias = jnp.where(ok[None], blocks, NEG_INF)
        out.append(bias.reshape(rp.shape[0], C_KROWS * GRID_W, C_QROWS * GRID_W))
    return jnp.stack(out)


def _nbr(k, qt, vt, rpb):
    b, g, dk, s = qt.shape
    tq = C_QROWS * GRID_W
    win = C_KROWS * GRID_W
    nq = s // tq
    bias = _nbr_bias(rpb, s // GRID_W)

    def variant(i):
        return jnp.where(i == 0, 0, jnp.where(i == nq - 1, 2, 1))

    return pl.pallas_call(
        functools.partial(_nbr_kernel, s_len=s, win=win),
        grid=(b, nq),
        scratch_shapes=[pltpu.VMEM((g, win, tq), F32)],
        in_specs=[
            pl.BlockSpec((1, g, win, tq), lambda bi, i: (variant(i), 0, 0, 0)),
            pl.BlockSpec((1, 1, s, k.shape[3]), lambda bi, i: (bi, 0, 0, 0)),
            pl.BlockSpec((1, g, dk, tq), lambda bi, i: (bi, 0, 0, i)),
            pl.BlockSpec((1, g, V_ROWS, s), lambda bi, i: (bi, 0, 0, 0)),
        ],
        out_specs=pl.BlockSpec((1, g, 64, tq), lambda bi, i: (bi, 0, 0, i)),
        out_shape=jax.ShapeDtypeStruct((b, g, 64, s), F32),
        compiler_params=pltpu.CompilerParams(
            dimension_semantics=("arbitrary", "arbitrary"), vmem_limit_bytes=VMEM_LIMIT),
        name="nbr",
    )(bias, k, qt, vt)


def _out_kernel(oa_ref, ob_ref, oc_ref, od_ref, sg_ref, h_ref, p_ref, lam_ref, sub_ref,
                wo_ref, pg_ref, wg_ref, wp_ref, fg_ref, y_ref, *, lam_init, final):
    ple = _dot(p_ref[0].astype(BF16), wp_ref[...])
    lv = lam_ref[...]
    lam = (jnp.exp(jnp.sum(lv[0:1] * lv[1:2], axis=-1, keepdims=True))
           - jnp.exp(jnp.sum(lv[2:3] * lv[3:4], axis=-1, keepdims=True)) + lam_init)
    parts = []
    for h in range(A_HEADS):
        o = oa_ref[0, 2 * h] - lam * oa_ref[0, 2 * h + 1]
        on = (o * lax.rsqrt(jnp.mean(o * o, axis=0, keepdims=True) + EPS)) * sub_ref[...]
        parts.append(on * (1.0 - lam_init))
    for ref in (ob_ref, oc_ref, od_ref):
        for h in range(4):
            parts.append(ref[0, h])
    mix = (jnp.concatenate(parts, axis=0) * sg_ref[0]).astype(BF16)
    h1 = h_ref[0] + _tn(mix, wo_ref[...])
    hn = ((h1 * lax.rsqrt(jnp.mean(h1 * h1, axis=-1, keepdims=True) + EPS)) * pg_ref[...]).astype(BF16)
    gate = jax.nn.sigmoid(_dot(hn, wg_ref[...]))
    h2 = h1 + gate * ple
    if final:
        h2 = (h2 * lax.rsqrt(jnp.mean(h2 * h2, axis=-1, keepdims=True) + EPS)) * fg_ref[...]
    y_ref[0] = h2


def _out(oa, ob, oc, od, sg, h, p, a_lambda, subln_g, w_out, ple_norm_g, w_ple_gate, w_ple_proj,
         final_norm_g, lam_init, final):
    b, s, _ = h.shape
    tm = min(TM, s)

    def heads(n):
        return pl.BlockSpec((1, n, 64, tm), lambda bi, i: (bi, 0, 0, i))

    return pl.pallas_call(
        functools.partial(_out_kernel, lam_init=lam_init, final=final),
        grid=(b, s // tm),
        in_specs=[
            heads(8), heads(4), heads(4), heads(4),
            pl.BlockSpec((1, D_MIX, tm), lambda bi, i: (bi, 0, i)),
            pl.BlockSpec((1, tm, D_MODEL), lambda bi, i: (bi, i, 0)),
            pl.BlockSpec((1, tm, D_PLE), lambda bi, i: (bi, i, 0)),
            _const_spec((4, A_QK_DIM)), _const_spec((A_V_DIM, 1)),
            _const_spec((D_MIX, D_MODEL)), _const_spec((1, D_MODEL)),
            _const_spec((D_MODEL, D_MODEL)), _const_spec((D_PLE, D_MODEL)), _const_spec((1, D_MODEL)),
        ],
        out_specs=pl.BlockSpec((1, tm, D_MODEL), lambda bi, i: (bi, i, 0)),
        out_shape=jax.ShapeDtypeStruct((b, s, D_MODEL), F32),
        compiler_params=pltpu.CompilerParams(
            dimension_semantics=("arbitrary", "arbitrary"), vmem_limit_bytes=VMEM_LIMIT),
        name="out",
    )(oa, ob, oc, od, sg, h, p, a_lambda.astype(F32), subln_g.reshape(A_V_DIM, 1).astype(F32),
      w_out.astype(BF16), ple_norm_g.reshape(1, D_MODEL), w_ple_gate.astype(BF16), w_ple_proj.astype(BF16),
      final_norm_g.reshape(1, D_MODEL))


def _trunk(x, p, norm_g, w_in, a_lambda, a_subln_g, b_sink, c_rpb, d_q_norm_g, d_kv_norm_g,
           d_w_uq, d_w_ukv, w_out, ple_norm_g, w_ple_gate, w_ple_proj, final_norm_g):
    depth = w_in.shape[0]
    s = x.shape[1]
    tabs = _rope_tables(s)
    h = x
    for i in range(depth):
        prep = _prep_layer(w_in[i], d_w_uq[i], d_w_ukv[i])
        (qa, ka, va, qb, kb, vb, qc, kc, vc, qd, kd, vd, sg) = _proj(
            h, norm_g[i], prep, tabs, d_q_norm_g[i], d_kv_norm_g[i])
        oa = _flash(ka, qa, va, lambda g: (0, 0, g // 4), lambda g: g // 2)
        od = _flash(kd, qd, vd, lambda g: (0, 0, g), lambda g: g)
        ob = _window(kb, qb, vb, b_sink[i])
        oc = _nbr(kc, qc, vc, c_rpb[i])
        lam_init = 0.8 - 0.6 * math.exp(-0.3 * i)
        h = _out(oa, ob, oc, od, sg, h, p[i], a_lambda[i], a_subln_g[i], w_out[i], ple_norm_g[i],
                 w_ple_gate[i], w_ple_proj[i], final_norm_g, lam_init, i == depth - 1)
    return h


def kernel(x_prompt, x_sample, p_prompt, p_sample, norm_g, w_in, a_lambda, a_subln_g, b_sink, c_rpb,
           d_q_norm_g, d_kv_norm_g, d_w_uq, d_w_ukv, w_out, ple_norm_g, w_ple_gate, w_ple_proj,
           final_norm_g):
    weights = (norm_g, w_in, a_lambda, a_subln_g, b_sink, c_rpb, d_q_norm_g, d_kv_norm_g, d_w_uq, d_w_ukv,
               w_out, ple_norm_g, w_ple_gate, w_ple_proj, final_norm_g)
    return (_trunk(x_prompt, p_prompt, *weights), _trunk(x_sample, p_sample, *weights))
```

```python
import functools
import math

import jax
import jax.numpy as jnp
import numpy as np
from jax import lax
from jax.experimental import pallas as pl
from jax.experimental.pallas import tpu as pltpu

F32 = jnp.float32
BF16 = jnp.bfloat16

D_MODEL = 1024
D_MIX = 1024
D_PLE = 256
GRID_W = 64
ROPE_THETA = 10000.0
EPS = 1e-6
NEG_INF = -1e30

A_HEADS, A_QK_DIM, A_V_DIM = 4, 32, 64
B_HEADS, B_KV_HEADS, B_HEAD_DIM, B_WINDOW = 4, 2, 64, 128
C_HEADS, C_HEAD_DIM, C_MAX_ROWS, C_WIN_COLS = 4, 64, 8, 16
D_HEADS, D_Q_RANK, D_KV_RANK, D_NOPE, D_ROPE, D_V = 4, 256, 128, 64, 32, 64

IN_SIZES = (256, 256, 256, 256, 128, 128, 256, 256, 256, D_Q_RANK, D_KV_RANK, D_ROPE, D_MIX)

V_ROWS = 80
VMEM_LIMIT = 56 * 1024 * 1024

TM = 512
TQ = 256
TQ_DENSE = 512
LOG2E = math.log2(math.e)
D_SCALE = (D_NOPE + D_ROPE) ** -0.5 * LOG2E
TK = 256
FLASH_QBLOCKS = 4
FLASH_BUF_BYTES = 24 * 1024 * 1024
C_QROWS = 4
C_KROWS = 12


def _nt(a, b):
    return lax.dot_general(a, b, (((1,), (1,)), ((), ())), preferred_element_type=F32)


def _tn(a, b):
    return lax.dot_general(a, b, (((0,), (0,)), ((), ())), preferred_element_type=F32)


def _dot(a, b):
    return jnp.dot(a, b, preferred_element_type=F32)


T_AQ, T_BQ, T_CQ, T_DCQ, T_DCKV, T_AV, T_BV, T_CV, T_GATE, T_END = (
    0, 256, 512, 768, 1024, 1152, 1408, 1536, 1792, 2816)
S_AK, S_BK, S_CK, S_DCKV, S_KPE, S_END = (0, 256, 384, 640, 768, 896)


def _swap_row_halves(x, d):
    h = d // 2
    return jnp.concatenate([x[r + o:r + o + h] for r in range(0, x.shape[0], d) for o in (h, 0)], axis=0)


def _swap_lane_halves(x, d):
    n = x.shape[1]
    lane = lax.broadcasted_iota(jnp.int32, x.shape, 1)
    return jnp.where(lane % d < d // 2, pltpu.roll(x, n - d // 2, 1), pltpu.roll(x, d // 2, 1))


def _proj_kernel(h_ref, g_ref, wt_ref, ws_ref, ta_ref, tb_ref, td_ref, sa_ref, sb_ref, sd_ref,
                 qg_ref, kvgc_ref, kvgr_ref, wqn_ref, wqp_ref, wkn_ref, place_ref, wvt_ref,
                 qa_ref, ka_ref, va_ref, qb_ref, kb_ref, vb_ref, qc_ref, kc_ref, vc_ref,
                 qd_ref, kd_ref, vd_ref, sg_ref):
    x = h_ref[0]
    tm = x.shape[0]
    ms = jnp.mean(x * x, axis=-1, keepdims=True)
    hn = ((x * lax.rsqrt(ms + EPS)) * g_ref[...]).astype(BF16)

    zs = _dot(hn, ws_ref[...])
    zak, zbk, zpe = zs[:, S_AK:S_BK], zs[:, S_BK:S_CK], zs[:, S_KPE:S_END]
    ka_ref[0, 0] = (zak * sa_ref[0] + _swap_lane_halves(zak, A_QK_DIM) * sa_ref[1]).astype(BF16)
    kb_ref[0, 0] = (zbk * sb_ref[0] + _swap_lane_halves(zbk, B_HEAD_DIM) * sb_ref[1]).astype(BF16)
    kc_ref[0, 0] = zs[:, S_CK:S_DCKV].astype(BF16)
    ck = zs[:, S_DCKV:S_KPE]
    ckn = ((ck * lax.rsqrt(jnp.mean(ck * ck, axis=-1, keepdims=True) + EPS)) * kvgr_ref[...]).astype(BF16)
    kpe = (zpe * sd_ref[0] + _swap_lane_halves(zpe, D_ROPE) * sd_ref[1]).astype(BF16)
    kd_ref[0, 0] = (_dot(ckn, wkn_ref[...]) + _dot(kpe, place_ref[...])).astype(BF16)

    zq = _nt(wt_ref[T_AQ:T_AV, :], hn)
    zv = _nt(wt_ref[T_AV:T_GATE, :], hn)
    gt = _nt(wt_ref[T_GATE:T_END, :], hn)
    sg_ref[0] = gt * jax.nn.sigmoid(gt)

    ones_rows = (lax.broadcasted_iota(jnp.int32, (V_ROWS - 64, tm), 0) == 0).astype(BF16)

    def store_values(ref, v, heads):
        for h in range(heads):
            ref[0, h, 0:64, :] = v[64 * h:64 * h + 64].astype(BF16)
            ref[0, h, 64:V_ROWS, :] = ones_rows

    za = zq[T_AQ:T_BQ]
    qa = za * ta_ref[0] + _swap_row_halves(za, A_QK_DIM) * ta_ref[1]
    for g in range(2 * A_HEADS):
        r = 32 * (g % 4)
        qa_ref[0, g] = jnp.zeros((128, tm), BF16)
        qa_ref[0, g, r:r + 32, :] = qa[32 * g:32 * g + 32].astype(BF16)
    store_values(va_ref, zv[0:T_BV - T_AV], A_HEADS)

    zb = zq[T_BQ:T_CQ]
    qb = zb * tb_ref[0] + _swap_row_halves(zb, B_HEAD_DIM) * tb_ref[1]
    for h in range(B_HEADS):
        r = 64 * (h // 2)
        qb_ref[0, h] = jnp.zeros((128, tm), BF16)
        qb_ref[0, h, r:r + 64, :] = qb[64 * h:64 * h + 64].astype(BF16)
    store_values(vb_ref, zv[T_BV - T_AV:T_CV - T_AV], B_KV_HEADS)

    qc = zq[T_CQ:T_DCQ] * (C_HEAD_DIM ** -0.5 * LOG2E)
    for h in range(C_HEADS):
        r = 64 * (h % 2)
        qc_ref[0, h] = jnp.zeros((128, tm), BF16)
        qc_ref[0, h, r:r + 64, :] = qc[64 * h:64 * h + 64].astype(BF16)
    store_values(vc_ref, zv[T_CV - T_AV:T_GATE - T_AV], C_HEADS)

    cq = zq[T_DCQ:T_DCKV]
    cqn = ((cq * lax.rsqrt(jnp.mean(cq * cq, axis=0, keepdims=True) + EPS)) * qg_ref[...]).astype(BF16)
    for h in range(D_HEADS):
        qn = _dot(wqn_ref[h], cqn) * D_SCALE
        qp = _dot(wqp_ref[h], cqn)
        qpe = qp * td_ref[0] + _swap_row_halves(qp, D_ROPE) * td_ref[1]
        qd_ref[0, h, 0:64, :] = qn.astype(BF16)
        qd_ref[0, h, 64:96, :] = qpe.astype(BF16)
        qd_ref[0, h, 96:128, :] = jnp.zeros((32, tm), BF16)
    ckt = zq[T_DCKV:T_AV]
    cktn = ((ckt * lax.rsqrt(jnp.mean(ckt * ckt, axis=0, keepdims=True) + EPS)) * kvgc_ref[...]).astype(BF16)
    store_values(vd_ref, _dot(wvt_ref[...], cktn), D_HEADS)


def _rope_cs(s, dim):
    inv = 1.0 / (ROPE_THETA ** (jnp.arange(0, dim, 2, dtype=F32) / dim))
    ang = jnp.arange(s, dtype=F32)[:, None] * inv[None, :]
    cos, sin = jnp.cos(ang), jnp.sin(ang)
    return jnp.concatenate([cos, cos], -1), jnp.concatenate([-sin, sin], -1)


def _rope_tables(s):
    c32, s32 = _rope_cs(s, A_QK_DIM)
    c64, s64 = _rope_cs(s, B_HEAD_DIM)
    a_scale, b_scale, d_scale = A_QK_DIM ** -0.5 * LOG2E, B_HEAD_DIM ** -0.5 * LOG2E, D_SCALE
    ta = jnp.stack([jnp.tile(c32, (1, 8)).T, jnp.tile(s32, (1, 8)).T]) * a_scale
    tb = jnp.stack([jnp.tile(c64, (1, 4)).T, jnp.tile(s64, (1, 4)).T]) * b_scale
    td = jnp.stack([c32.T, s32.T]) * d_scale
    sa = jnp.stack([jnp.tile(c32, (1, 8)), jnp.tile(s32, (1, 8))])
    sb = jnp.stack([jnp.tile(c64, (1, 2)), jnp.tile(s64, (1, 2))])
    pad = jnp.zeros((s, 128 - D_ROPE), F32)
    sd = jnp.stack([jnp.concatenate([c32, pad], -1), jnp.concatenate([s32, pad], -1)])
    return ta, tb, td, sa, sb, sd


def _prep_layer(w_in, d_w_uq, d_w_ukv):
    pts = [0]
    for n in IN_SIZES:
        pts.append(pts[-1] + n)
    (aq, ak, av, bq, bk, bv, cq, ck, cv, dcq, dckv, dkr, gate) = [w_in[:, pts[i]:pts[i + 1]] for i in range(13)]
    wt = jnp.concatenate([aq, bq, cq, dcq, dckv, av, bv, cv, gate], axis=1).T.astype(BF16)
    zpad = jnp.zeros((D_MODEL, 128 - D_ROPE), F32)
    ws = jnp.concatenate([ak, bk, ck, dckv, dkr, zpad], axis=1).astype(BF16)
    uq = d_w_uq.reshape(D_Q_RANK, D_HEADS, D_NOPE + D_ROPE)
    uq_n, uq_p = uq[:, :, :D_NOPE], uq[:, :, D_NOPE:]
    wqn = jnp.transpose(uq_n, (1, 2, 0))
    wqp = jnp.transpose(uq_p, (1, 2, 0))
    ukv = d_w_ukv.reshape(D_KV_RANK, D_HEADS, D_NOPE + D_V)
    wkn = jnp.pad(ukv[:, :, :D_NOPE], ((0, 0), (0, 0), (0, 128 - D_NOPE))).reshape(D_KV_RANK, D_HEADS * 128)
    place = np.zeros((128, D_HEADS, 128), np.float32)
    place[np.arange(D_ROPE), :, D_NOPE + np.arange(D_ROPE)] = 1.0
    place = jnp.asarray(place.reshape(128, D_HEADS * 128), BF16)
    wvt = ukv[:, :, D_NOPE:].reshape(D_KV_RANK, D_HEADS * D_V).T
    return (wt, ws, wqn.astype(BF16), wqp.astype(BF16), wkn.astype(BF16), place, wvt.astype(BF16))


def _const_spec(shape):
    nd = len(shape)
    return pl.BlockSpec(shape, lambda *_: (0,) * nd)


def _proj(h, norm_g, prep, tabs, q_norm_g, kv_norm_g):
    b, s, _ = h.shape
    tm = min(TM, s)
    wt, ws, wqn, wqp, wkn, place, wvt = prep
    ta, tb, td, sa, sb, sd = tabs

    def tok(shape_tail):
        return pl.BlockSpec((1, 1, tm) + shape_tail, lambda i, j: (j, 0, i, 0))

    def chan(groups, rows):
        return pl.BlockSpec((1, groups, rows, tm), lambda i, j: (j, 0, 0, i))

    in_specs = [
        pl.BlockSpec((1, tm, D_MODEL), lambda i, j: (j, i, 0)),
        _const_spec((1, D_MODEL)), _const_spec(wt.shape), _const_spec(ws.shape),
        pl.BlockSpec((2, 256, tm), lambda i, j: (0, 0, i)),
        pl.BlockSpec((2, 256, tm), lambda i, j: (0, 0, i)),
        pl.BlockSpec((2, D_ROPE, tm), lambda i, j: (0, 0, i)),
        pl.BlockSpec((2, tm, 256), lambda i, j: (0, i, 0)),
        pl.BlockSpec((2, tm, 128), lambda i, j: (0, i, 0)),
        pl.BlockSpec((2, tm, 128), lambda i, j: (0, i, 0)),
        _const_spec((D_Q_RANK, 1)), _const_spec((D_KV_RANK, 1)), _const_spec((1, D_KV_RANK)),
        _const_spec(wqn.shape), _const_spec(wqp.shape),
        _const_spec(wkn.shape), _const_spec(place.shape), _const_spec(wvt.shape),
    ]
    out_shape = [
        jax.ShapeDtypeStruct((b, 8, 128, s), BF16),
        jax.ShapeDtypeStruct((b, 1, s, 256), BF16),
        jax.ShapeDtypeStruct((b, A_HEADS, V_ROWS, s), BF16),
        jax.ShapeDtypeStruct((b, B_HEADS, 128, s), BF16),
        jax.ShapeDtypeStruct((b, 1, s, 128), BF16),
        jax.ShapeDtypeStruct((b, B_KV_HEADS, V_ROWS, s), BF16),
        jax.ShapeDtypeStruct((b, C_HEADS, 128, s), BF16),
        jax.ShapeDtypeStruct((b, 1, s, 256), BF16),
        jax.ShapeDtypeStruct((b, C_HEADS, V_ROWS, s), BF16),
        jax.ShapeDtypeStruct((b, D_HEADS, 128, s), BF16),
        jax.ShapeDtypeStruct((b, 1, s, D_HEADS * 128), BF16),
        jax.ShapeDtypeStruct((b, D_HEADS, V_ROWS, s), BF16),
        jax.ShapeDtypeStruct((b, D_MIX, s), F32),
    ]
    out_specs = [
        chan(8, 128), tok((256,)), chan(A_HEADS, V_ROWS),
        chan(B_HEADS, 128), tok((128,)), chan(B_KV_HEADS, V_ROWS),
        chan(C_HEADS, 128), tok((256,)), chan(C_HEADS, V_ROWS),
        chan(D_HEADS, 128), tok((D_HEADS * 128,)), chan(D_HEADS, V_ROWS),
        pl.BlockSpec((1, D_MIX, tm), lambda i, j: (j, 0, i)),
    ]
    return pl.pallas_call(
        _proj_kernel,
        grid=(s // tm, b),
        in_specs=in_specs, out_specs=out_specs, out_shape=out_shape,
        compiler_params=pltpu.CompilerParams(
            dimension_semantics=("arbitrary", "arbitrary"), vmem_limit_bytes=VMEM_LIMIT),
        name="proj",
    )(h, norm_g.reshape(1, D_MODEL), wt, ws, ta, tb, td, sa, sb, sd,
      q_norm_g.reshape(D_Q_RANK, 1), kv_norm_g.reshape(D_KV_RANK, 1), kv_norm_g.reshape(1, D_KV_RANK),
      wqn, wqp, wkn, place, wvt)


def _flash_kernel(k_ref, q_ref, v_ref, o_ref, s_buf, p_buf, pv_buf, *, tk, nk, tq):
    total = (q_ref.shape[3] // tq) * nk

    def scores(t):
        b, j = divmod(t, nk)
        s = _dot(k_ref[0, 0, j * tk:(j + 1) * tk, :], q_ref[0, 0, :, b * tq:(b + 1) * tq])
        s_buf[j] = s
        return jnp.max(s, axis=0, keepdims=True)

    def softmax(t, m, cmax):
        j = t % nk
        m_new = jnp.maximum(m, cmax)
        p_buf[j] = jnp.exp2(s_buf[j] - m_new).astype(BF16)
        return m_new, jnp.exp2(m - m_new)

    def values(t, acc, alpha):
        j = t % nk
        pv_buf[...] = _dot(v_ref[0, 0, :, j * tk:(j + 1) * tk], p_buf[j])
        return acc * alpha + pv_buf[...]

    m_init = jnp.full((1, tq), NEG_INF, F32)
    acc_init = jnp.zeros((V_ROWS, tq), F32)
    m, acc = m_init, acc_init
    cmax = {0: scores(0)}
    alpha = {}
    for t in range(total + 1):
        if t + 1 < total:
            cmax[t + 1] = scores(t + 1)
        if t < total:
            if t % nk == 0:
                m = m_init
            m, alpha[t] = softmax(t, m, cmax.pop(t))
        if t >= 1:
            if (t - 1) % nk == 0:
                acc = acc_init
            acc = values(t - 1, acc, alpha.pop(t - 1))
            if t % nk == 0:
                b = t // nk - 1
                o_ref[0, 0, :, b * tq:(b + 1) * tq] = acc[0:64] / acc[64:65]


def _flash(k, qt, vt, k_group, v_group):
    b, g, dk, s = qt.shape
    tq, tk = min(TQ_DENSE, s), min(TK, s)
    nk = s // tk
    slots = nk
    assert slots * tk * tq * 6 <= FLASH_BUF_BYTES
    tqs = tq * math.gcd(s // tq, FLASH_QBLOCKS)
    return pl.pallas_call(
        functools.partial(_flash_kernel, tk=tk, nk=nk, tq=tq),
        grid=(b, g, s // tqs),
        scratch_shapes=[pltpu.VMEM((slots, tk, tq), F32), pltpu.VMEM((slots, tk, tq), BF16),
                        pltpu.VMEM((V_ROWS, tq), F32)],
        in_specs=[
            pl.BlockSpec((1, 1, s, dk), lambda bi, gi, i: (bi,) + k_group(gi)),
            pl.BlockSpec((1, 1, dk, tqs), lambda bi, gi, i: (bi, gi, 0, i)),
            pl.BlockSpec((1, 1, V_ROWS, s), lambda bi, gi, i: (bi, v_group(gi), 0, 0)),
        ],
        out_specs=pl.BlockSpec((1, 1, 64, tqs), lambda bi, gi, i: (bi, gi, 0, i)),
        out_shape=jax.ShapeDtypeStruct((b, g, 64, s), F32),
        compiler_params=pltpu.CompilerParams(
            dimension_semantics=("arbitrary", "arbitrary", "arbitrary"), vmem_limit_bytes=VMEM_LIMIT),
        name="flash",
    )(k, qt, vt)


def _window_kernel(sink_ref, k_ref, q_ref, v_ref, o_ref, *, s_len, win):
    i = pl.program_id(1)
    tq = q_ref.shape[3]
    q0 = i * tq
    start = pl.multiple_of(jnp.clip(q0 - B_WINDOW, 0, s_len - win), 128)
    k = k_ref[0, 0, pl.ds(start, win), :]
    kpos = start + lax.broadcasted_iota(jnp.int32, (win, tq), 0)
    qpos = q0 + lax.broadcasted_iota(jnp.int32, (win, tq), 1)
    in_band = jnp.abs(kpos - qpos) <= B_WINDOW
    scores = [_dot(k, q_ref[0, h]) for h in range(B_HEADS)]
    for h in range(B_HEADS):
        s = jnp.where(in_band, scores[h], NEG_INF)
        sink = sink_ref[h] * LOG2E
        m = jnp.maximum(jnp.max(s, axis=0, keepdims=True), sink)
        p = jnp.exp2(s - m).astype(BF16)
        acc = _dot(v_ref[0, h // (B_HEADS // B_KV_HEADS), :, pl.ds(start, win)], p)
        o_ref[0, h] = acc[0:64] / (acc[64:65] + jnp.exp2(sink - m))


def _window(k, qt, vt, sink):
    b, g, dk, s = qt.shape
    tq = min(TQ, s)
    win = tq + 2 * B_WINDOW
    return pl.pallas_call(
        functools.partial(_window_kernel, s_len=s, win=win),
        grid=(b, s // tq),
        in_specs=[
            pl.BlockSpec(memory_space=pltpu.SMEM),
            pl.BlockSpec((1, 1, s, dk), lambda bi, i: (bi, 0, 0, 0)),
            pl.BlockSpec((1, g, dk, tq), lambda bi, i: (bi, 0, 0, i)),
            pl.BlockSpec((1, B_KV_HEADS, V_ROWS, s), lambda bi, i: (bi, 0, 0, 0)),
        ],
        out_specs=pl.BlockSpec((1, g, 64, tq), lambda bi, i: (bi, 0, 0, i)),
        out_shape=jax.ShapeDtypeStruct((b, g, 64, s), F32),
        compiler_params=pltpu.CompilerParams(
            dimension_semantics=("arbitrary", "arbitrary"), vmem_limit_bytes=VMEM_LIMIT),
        name="window",
    )(sink.astype(F32), k, qt, vt)


def _nbr_kernel(bias_ref, k_ref, q_ref, v_ref, o_ref, s_buf, *, s_len, win):
    i = pl.program_id(1)
    tq = q_ref.shape[3]
    start = pl.multiple_of(jnp.clip(i * tq - (C_MAX_ROWS // 2) * GRID_W, 0, s_len - win), 256)
    for h in range(C_HEADS):
        k = k_ref[0, 0, pl.ds(start, win), 128 * (h // 2):128 * (h // 2) + 128]
        s_buf[h] = _dot(k, q_ref[0, h])
    for h in range(C_HEADS):
        s = s_buf[h] + bias_ref[0, h]
        m = jnp.max(s, axis=0, keepdims=True)
        p = jnp.exp2(s - m).astype(BF16)
        acc = _dot(v_ref[0, h, :, pl.ds(start, win)], p)
        o_ref[0, h] = acc[0:64] / acc[64:65]


def _nbr_bias(rpb, rows):
    col = np.arange(GRID_W)
    cs = np.clip(col - C_WIN_COLS // 2, 0, GRID_W - C_WIN_COLS)
    col_ok = (col[:, None] >= cs[None, :]) & (col[:, None] < cs[None, :] + C_WIN_COLS)
    dc = np.clip(col[:, None] - col[None, :], -(C_WIN_COLS - 1), C_WIN_COLS - 1) + (C_WIN_COLS - 1)
    rp = rpb.astype(F32)
    by_cols = jnp.zeros(rp.shape[:2] + dc.shape, F32)
    for b in range(rp.shape[2]):
        by_cols = by_cols + jnp.where((dc == b)[None, None], rp[:, :, b][:, :, None, None], 0.0)
    kr = min(C_MAX_ROWS, rows)
    out = []
    for r0 in (0, C_QROWS, rows - C_QROWS):
        rk0 = min(max(r0 - C_MAX_ROWS // 2, 0), rows - C_KROWS)
        rq = r0 + np.arange(C_QROWS)
        rk = rk0 + np.arange(C_KROWS)
        rs = np.clip(rq - kr // 2, 0, rows - kr)
        row_ok = (rk[:, None] >= rs[None, :]) & (rk[:, None] < rs[None, :] + kr)
        dr = np.clip(rk[:, None] - rq[None, :] + (C_MAX_ROWS - 1), 0, 2 * C_MAX_ROWS - 2)
        blocks = jnp.stack([jnp.stack([by_cols[:, dr[a, c]] for c in range(C_QROWS)], axis=2)
                            for a in range(C_KROWS)], axis=1)
        ok = row_ok[:, None, :, None] & col_ok[None, :, None, :]
        bias = jnp.where(ok[None], blocks * LOG2E, NEG_INF)
        out.append(bias.reshape(rp.shape[0], C_KROWS * GRID_W, C_QROWS * GRID_W))
    return jnp.stack(out)


def _nbr(k, qt, vt, rpb):
    b, g, dk, s = qt.shape
    tq = C_QROWS * GRID_W
    win = C_KROWS * GRID_W
    nq = s // tq
    bias = _nbr_bias(rpb, s // GRID_W)

    def variant(i):
        return jnp.where(i == 0, 0, jnp.where(i == nq - 1, 2, 1))

    return pl.pallas_call(
        functools.partial(_nbr_kernel, s_len=s, win=win),
        grid=(b, nq),
        scratch_shapes=[pltpu.VMEM((g, win, tq), F32)],
        in_specs=[
            pl.BlockSpec((1, g, win, tq), lambda bi, i: (variant(i), 0, 0, 0)),
            pl.BlockSpec((1, 1, s, k.shape[3]), lambda bi, i: (bi, 0, 0, 0)),
            pl.BlockSpec((1, g, dk, tq), lambda bi, i: (bi, 0, 0, i)),
            pl.BlockSpec((1, g, V_ROWS, s), lambda bi, i: (bi, 0, 0, 0)),
        ],
        out_specs=pl.BlockSpec((1, g, 64, tq), lambda bi, i: (bi, 0, 0, i)),
        out_shape=jax.ShapeDtypeStruct((b, g, 64, s), F32),
        compiler_params=pltpu.CompilerParams(
            dimension_semantics=("arbitrary", "arbitrary"), vmem_limit_bytes=VMEM_LIMIT),
        name="nbr",
    )(bias, k, qt, vt)


def _out_kernel(oa_ref, ob_ref, oc_ref, od_ref, sg_ref, h_ref, p_ref, lam_ref, sub_ref,
                wo_ref, pg_ref, wg_ref, wp_ref, fg_ref, y_ref, *, lam_init, final):
    ple = _dot(p_ref[0].astype(BF16), wp_ref[...])
    lv = lam_ref[...]
    lam = (jnp.exp(jnp.sum(lv[0:1] * lv[1:2], axis=-1, keepdims=True))
           - jnp.exp(jnp.sum(lv[2:3] * lv[3:4], axis=-1, keepdims=True)) + lam_init)
    parts = []
    for h in range(A_HEADS):
        o = oa_ref[0, 2 * h] - lam * oa_ref[0, 2 * h + 1]
        on = (o * lax.rsqrt(jnp.mean(o * o, axis=0, keepdims=True) + EPS)) * sub_ref[...]
        parts.append(on * (1.0 - lam_init))
    for ref in (ob_ref, oc_ref, od_ref):
        for h in range(4):
            parts.append(ref[0, h])
    mix = (jnp.concatenate(parts, axis=0) * sg_ref[0]).astype(BF16)
    h1 = h_ref[0] + _tn(mix, wo_ref[...])
    hn = ((h1 * lax.rsqrt(jnp.mean(h1 * h1, axis=-1, keepdims=True) + EPS)) * pg_ref[...]).astype(BF16)
    gate = jax.nn.sigmoid(_dot(hn, wg_ref[...]))
    h2 = h1 + gate * ple
    if final:
        h2 = (h2 * lax.rsqrt(jnp.mean(h2 * h2, axis=-1, keepdims=True) + EPS)) * fg_ref[...]
    y_ref[0] = h2


def _out(oa, ob, oc, od, sg, h, p, a_lambda, subln_g, w_out, ple_norm_g, w_ple_gate, w_ple_proj,
         final_norm_g, lam_init, final):
    b, s, _ = h.shape
    tm = min(TM, s)

    def heads(n):
        return pl.BlockSpec((1, n, 64, tm), lambda bi, i: (bi, 0, 0, i))

    return pl.pallas_call(
        functools.partial(_out_kernel, lam_init=lam_init, final=final),
        grid=(b, s // tm),
        in_specs=[
            heads(8), heads(4), heads(4), heads(4),
            pl.BlockSpec((1, D_MIX, tm), lambda bi, i: (bi, 0, i)),
            pl.BlockSpec((1, tm, D_MODEL), lambda bi, i: (bi, i, 0)),
            pl.BlockSpec((1, tm, D_PLE), lambda bi, i: (bi, i, 0)),
            _const_spec((4, A_QK_DIM)), _const_spec((A_V_DIM, 1)),
            _const_spec((D_MIX, D_MODEL)), _const_spec((1, D_MODEL)),
            _const_spec((D_MODEL, D_MODEL)), _const_spec((D_PLE, D_MODEL)), _const_spec((1, D_MODEL)),
        ],
        out_specs=pl.BlockSpec((1, tm, D_MODEL), lambda bi, i: (bi, i, 0)),
        out_shape=jax.ShapeDtypeStruct((b, s, D_MODEL), F32),
        compiler_params=pltpu.CompilerParams(
            dimension_semantics=("arbitrary", "arbitrary"), vmem_limit_bytes=VMEM_LIMIT),
        name="out",
    )(oa, ob, oc, od, sg, h, p, a_lambda.astype(F32), subln_g.reshape(A_V_DIM, 1).astype(F32),
      w_out.astype(BF16), ple_norm_g.reshape(1, D_MODEL), w_ple_gate.astype(BF16), w_ple_proj.astype(BF16),
      final_norm_g.reshape(1, D_MODEL))


def _trunk(x, p, tabs, preps, norm_g, a_lambda, a_subln_g, b_sink, c_rpb, d_q_norm_g, d_kv_norm_g,
           w_out, ple_norm_g, w_ple_gate, w_ple_proj, final_norm_g):
    depth = len(preps)
    s = x.shape[1]
    ta, tb, td, sa, sb, sd = tabs
    tabs = (ta[:, :, :s], tb[:, :, :s], td[:, :, :s], sa[:, :s], sb[:, :s], sd[:, :s])
    h = x
    for i in range(depth):
        (qa, ka, va, qb, kb, vb, qc, kc, vc, qd, kd, vd, sg) = _proj(
            h, norm_g[i], preps[i], tabs, d_q_norm_g[i], d_kv_norm_g[i])
        oa = _flash(ka, qa, va, lambda g: (0, 0, g // 4), lambda g: g // 2)
        od = _flash(kd, qd, vd, lambda g: (0, 0, g), lambda g: g)
        ob = _window(kb, qb, vb, b_sink[i])
        oc = _nbr(kc, qc, vc, c_rpb[i])
        lam_init = 0.8 - 0.6 * math.exp(-0.3 * i)
        h = _out(oa, ob, oc, od, sg, h, p[i], a_lambda[i], a_subln_g[i], w_out[i], ple_norm_g[i],
                 w_ple_gate[i], w_ple_proj[i], final_norm_g, lam_init, i == depth - 1)
    return h


def kernel(x_prompt, x_sample, p_prompt, p_sample, norm_g, w_in, a_lambda, a_subln_g, b_sink, c_rpb,
           d_q_norm_g, d_kv_norm_g, d_w_uq, d_w_ukv, w_out, ple_norm_g, w_ple_gate, w_ple_proj,
           final_norm_g):
    tabs = _rope_tables(max(x_prompt.shape[1], x_sample.shape[1]))
    preps = [_prep_layer(w_in[i], d_w_uq[i], d_w_ukv[i]) for i in range(w_in.shape[0])]
    weights = (norm_g, a_lambda, a_subln_g, b_sink, c_rpb, d_q_norm_g, d_kv_norm_g,
               w_out, ple_norm_g, w_ple_gate, w_ple_proj, final_norm_g)
    return (_trunk(x_prompt, p_prompt, tabs, preps, *weights), _trunk(x_sample, p_sample, tabs, preps, *weights))
```

```python
import functools
import math

import jax
import jax.numpy as jnp
import numpy as np
from jax import lax
from jax.experimental import pallas as pl
from jax.experimental.pallas import tpu as pltpu

F32 = jnp.float32
BF16 = jnp.bfloat16

D_MODEL = 1024
D_MIX = 1024
D_PLE = 256
GRID_W = 64
ROPE_THETA = 10000.0
EPS = 1e-6
NEG_INF = -1e30

A_HEADS, A_QK_DIM, A_V_DIM = 4, 32, 64
B_HEADS, B_KV_HEADS, B_HEAD_DIM, B_WINDOW = 4, 2, 64, 128
C_HEADS, C_HEAD_DIM, C_MAX_ROWS, C_WIN_COLS = 4, 64, 8, 16
D_HEADS, D_Q_RANK, D_KV_RANK, D_NOPE, D_ROPE, D_V = 4, 256, 128, 64, 32, 64

IN_SIZES = (256, 256, 256, 256, 128, 128, 256, 256, 256, D_Q_RANK, D_KV_RANK, D_ROPE, D_MIX)

V_ROWS = 80
VMEM_LIMIT = 56 * 1024 * 1024

TM = 512
TQ_DENSE = 512
LOG2E = math.log2(math.e)
D_SCALE = (D_NOPE + D_ROPE) ** -0.5 * LOG2E
TK = 256
FLASH_QBLOCKS = 4
FLASH_BUF_BYTES = 24 * 1024 * 1024
C_QROWS = 4
C_KROWS = 12


def _nt(a, b):
    return lax.dot_general(a, b, (((1,), (1,)), ((), ())), preferred_element_type=F32)


def _tn(a, b):
    return lax.dot_general(a, b, (((0,), (0,)), ((), ())), preferred_element_type=F32)


def _dot(a, b):
    return jnp.dot(a, b, preferred_element_type=F32)


T_AQ, T_BQ, T_CQ, T_DCQ, T_DCKV, T_AV, T_BV, T_CV, T_GATE, T_END = (
    0, 256, 512, 768, 1024, 1152, 1408, 1536, 1792, 2816)
S_AK, S_BK, S_CK, S_DCKV, S_KPE, S_END = (0, 256, 384, 640, 768, 896)


def _swap_row_halves(x, d):
    h = d // 2
    return jnp.concatenate([x[r + o:r + o + h] for r in range(0, x.shape[0], d) for o in (h, 0)], axis=0)


def _swap_lane_halves(x, d):
    n = x.shape[1]
    lane = lax.broadcasted_iota(jnp.int32, x.shape, 1)
    return jnp.where(lane % d < d // 2, pltpu.roll(x, n - d // 2, 1), pltpu.roll(x, d // 2, 1))


def _proj_kernel(h_ref, g_ref, wt_ref, ws_ref, ta_ref, tb_ref, td_ref, sa_ref, sb_ref, sd_ref,
                 qg_ref, kvgc_ref, kvgr_ref, wqn_ref, wqp_ref, wkn_ref, place_ref, wvt_ref,
                 qa_ref, ka_ref, va_ref, qb_ref, kb_ref, vb_ref, qc_ref, kc_ref, vc_ref,
                 qd_ref, kd_ref, vd_ref, sg_ref):
    x = h_ref[0]
    tm = x.shape[0]
    ms = jnp.mean(x * x, axis=-1, keepdims=True)
    hn = ((x * lax.rsqrt(ms + EPS)) * g_ref[...]).astype(BF16)

    zs = _dot(hn, ws_ref[...])
    zak, zbk, zpe = zs[:, S_AK:S_BK], zs[:, S_BK:S_CK], zs[:, S_KPE:S_END]
    ka_ref[0, 0] = (zak * sa_ref[0] + _swap_lane_halves(zak, A_QK_DIM) * sa_ref[1]).astype(BF16)
    kb_ref[0, 0] = (zbk * sb_ref[0] + _swap_lane_halves(zbk, B_HEAD_DIM) * sb_ref[1]).astype(BF16)
    kc_ref[0, 0] = zs[:, S_CK:S_DCKV].astype(BF16)
    ck = zs[:, S_DCKV:S_KPE]
    ckn = ((ck * lax.rsqrt(jnp.mean(ck * ck, axis=-1, keepdims=True) + EPS)) * kvgr_ref[...]).astype(BF16)
    kpe = (zpe * sd_ref[0] + _swap_lane_halves(zpe, D_ROPE) * sd_ref[1]).astype(BF16)
    kd_ref[0, 0] = (_dot(ckn, wkn_ref[...]) + _dot(kpe, place_ref[...])).astype(BF16)

    zq = _nt(wt_ref[T_AQ:T_AV, :], hn)
    zv = _nt(wt_ref[T_AV:T_GATE, :], hn)
    gt = _nt(wt_ref[T_GATE:T_END, :], hn)
    sg_ref[0] = gt * jax.nn.sigmoid(gt)

    ones_rows = (lax.broadcasted_iota(jnp.int32, (V_ROWS - 64, tm), 0) == 0).astype(BF16)

    def store_values(ref, v, heads):
        for h in range(heads):
            ref[0, h, 0:64, :] = v[64 * h:64 * h + 64].astype(BF16)
            ref[0, h, 64:V_ROWS, :] = ones_rows

    za = zq[T_AQ:T_BQ]
    qa = za * ta_ref[0] + _swap_row_halves(za, A_QK_DIM) * ta_ref[1]
    for g in range(2 * A_HEADS):
        r = 32 * (g % 4)
        qa_ref[0, g] = jnp.zeros((128, tm), BF16)
        qa_ref[0, g, r:r + 32, :] = qa[32 * g:32 * g + 32].astype(BF16)
    store_values(va_ref, zv[0:T_BV - T_AV], A_HEADS)

    zb = zq[T_BQ:T_CQ]
    qb = zb * tb_ref[0] + _swap_row_halves(zb, B_HEAD_DIM) * tb_ref[1]
    for h in range(B_HEADS):
        r = 64 * (h // 2)
        qb_ref[0, h] = jnp.zeros((128, tm), BF16)
        qb_ref[0, h, r:r + 64, :] = qb[64 * h:64 * h + 64].astype(BF16)
    store_values(vb_ref, zv[T_BV - T_AV:T_CV - T_AV], B_KV_HEADS)

    qc = zq[T_CQ:T_DCQ] * (C_HEAD_DIM ** -0.5 * LOG2E)
    for h in range(C_HEADS):
        r = 64 * (h % 2)
        qc_ref[0, h] = jnp.zeros((128, tm), BF16)
        qc_ref[0, h, r:r + 64, :] = qc[64 * h:64 * h + 64].astype(BF16)
    store_values(vc_ref, zv[T_CV - T_AV:T_GATE - T_AV], C_HEADS)

    cq = zq[T_DCQ:T_DCKV]
    cqn = ((cq * lax.rsqrt(jnp.mean(cq * cq, axis=0, keepdims=True) + EPS)) * qg_ref[...]).astype(BF16)
    for h in range(D_HEADS):
        qn = _dot(wqn_ref[h], cqn) * D_SCALE
        qp = _dot(wqp_ref[h], cqn)
        qpe = qp * td_ref[0] + _swap_row_halves(qp, D_ROPE) * td_ref[1]
        qd_ref[0, h, 0:64, :] = qn.astype(BF16)
        qd_ref[0, h, 64:96, :] = qpe.astype(BF16)
        qd_ref[0, h, 96:128, :] = jnp.zeros((32, tm), BF16)
    ckt = zq[T_DCKV:T_AV]
    cktn = ((ckt * lax.rsqrt(jnp.mean(ckt * ckt, axis=0, keepdims=True) + EPS)) * kvgc_ref[...]).astype(BF16)
    store_values(vd_ref, _dot(wvt_ref[...], cktn), D_HEADS)


def _rope_cs(s, dim):
    inv = 1.0 / (ROPE_THETA ** (jnp.arange(0, dim, 2, dtype=F32) / dim))
    ang = jnp.arange(s, dtype=F32)[:, None] * inv[None, :]
    cos, sin = jnp.cos(ang), jnp.sin(ang)
    return jnp.concatenate([cos, cos], -1), jnp.concatenate([-sin, sin], -1)


def _rope_tables(s):
    c32, s32 = _rope_cs(s, A_QK_DIM)
    c64, s64 = _rope_cs(s, B_HEAD_DIM)
    a_scale, b_scale, d_scale = A_QK_DIM ** -0.5 * LOG2E, B_HEAD_DIM ** -0.5 * LOG2E, D_SCALE
    ta = jnp.stack([jnp.tile(c32, (1, 8)).T, jnp.tile(s32, (1, 8)).T]) * a_scale
    tb = jnp.stack([jnp.tile(c64, (1, 4)).T, jnp.tile(s64, (1, 4)).T]) * b_scale
    td = jnp.stack([c32.T, s32.T]) * d_scale
    sa = jnp.stack([jnp.tile(c32, (1, 8)), jnp.tile(s32, (1, 8))])
    sb = jnp.stack([jnp.tile(c64, (1, 2)), jnp.tile(s64, (1, 2))])
    pad = jnp.zeros((s, 128 - D_ROPE), F32)
    sd = jnp.stack([jnp.concatenate([c32, pad], -1), jnp.concatenate([s32, pad], -1)])
    return ta, tb, td, sa, sb, sd


def _prep_layer(w_in, d_w_uq, d_w_ukv):
    pts = [0]
    for n in IN_SIZES:
        pts.append(pts[-1] + n)
    (aq, ak, av, bq, bk, bv, cq, ck, cv, dcq, dckv, dkr, gate) = [w_in[:, pts[i]:pts[i + 1]] for i in range(13)]
    wt = jnp.concatenate([aq, bq, cq, dcq, dckv, av, bv, cv, gate], axis=1).T.astype(BF16)
    zpad = jnp.zeros((D_MODEL, 128 - D_ROPE), F32)
    ws = jnp.concatenate([ak, bk, ck, dckv, dkr, zpad], axis=1).astype(BF16)
    uq = d_w_uq.reshape(D_Q_RANK, D_HEADS, D_NOPE + D_ROPE)
    uq_n, uq_p = uq[:, :, :D_NOPE], uq[:, :, D_NOPE:]
    wqn = jnp.transpose(uq_n, (1, 2, 0))
    wqp = jnp.transpose(uq_p, (1, 2, 0))
    ukv = d_w_ukv.reshape(D_KV_RANK, D_HEADS, D_NOPE + D_V)
    wkn = jnp.pad(ukv[:, :, :D_NOPE], ((0, 0), (0, 0), (0, 128 - D_NOPE))).reshape(D_KV_RANK, D_HEADS * 128)
    place = np.zeros((128, D_HEADS, 128), np.float32)
    place[np.arange(D_ROPE), :, D_NOPE + np.arange(D_ROPE)] = 1.0
    place = jnp.asarray(place.reshape(128, D_HEADS * 128), BF16)
    wvt = ukv[:, :, D_NOPE:].reshape(D_KV_RANK, D_HEADS * D_V).T
    return (wt, ws, wqn.astype(BF16), wqp.astype(BF16), wkn.astype(BF16), place, wvt.astype(BF16))


def _const_spec(shape):
    nd = len(shape)
    return pl.BlockSpec(shape, lambda *_: (0,) * nd)


def _proj(h, norm_g, prep, tabs, q_norm_g, kv_norm_g):
    b, s, _ = h.shape
    tm = min(TM, s)
    wt, ws, wqn, wqp, wkn, place, wvt = prep
    ta, tb, td, sa, sb, sd = tabs

    def tok(shape_tail):
        return pl.BlockSpec((1, 1, tm) + shape_tail, lambda i, j: (j, 0, i, 0))

    def chan(groups, rows):
        return pl.BlockSpec((1, groups, rows, tm), lambda i, j: (j, 0, 0, i))

    in_specs = [
        pl.BlockSpec((1, tm, D_MODEL), lambda i, j: (j, i, 0)),
        _const_spec((1, D_MODEL)), _const_spec(wt.shape), _const_spec(ws.shape),
        pl.BlockSpec((2, 256, tm), lambda i, j: (0, 0, i)),
        pl.BlockSpec((2, 256, tm), lambda i, j: (0, 0, i)),
        pl.BlockSpec((2, D_ROPE, tm), lambda i, j: (0, 0, i)),
        pl.BlockSpec((2, tm, 256), lambda i, j: (0, i, 0)),
        pl.BlockSpec((2, tm, 128), lambda i, j: (0, i, 0)),
        pl.BlockSpec((2, tm, 128), lambda i, j: (0, i, 0)),
        _const_spec((D_Q_RANK, 1)), _const_spec((D_KV_RANK, 1)), _const_spec((1, D_KV_RANK)),
        _const_spec(wqn.shape), _const_spec(wqp.shape),
        _const_spec(wkn.shape), _const_spec(place.shape), _const_spec(wvt.shape),
    ]
    out_shape = [
        jax.ShapeDtypeStruct((b, 8, 128, s), BF16),
        jax.ShapeDtypeStruct((b, 1, s, 256), BF16),
        jax.ShapeDtypeStruct((b, A_HEADS, V_ROWS, s), BF16),
        jax.ShapeDtypeStruct((b, B_HEADS, 128, s), BF16),
        jax.ShapeDtypeStruct((b, 1, s, 128), BF16),
        jax.ShapeDtypeStruct((b, B_KV_HEADS, V_ROWS, s), BF16),
        jax.ShapeDtypeStruct((b, C_HEADS, 128, s), BF16),
        jax.ShapeDtypeStruct((b, 1, s, 256), BF16),
        jax.ShapeDtypeStruct((b, C_HEADS, V_ROWS, s), BF16),
        jax.ShapeDtypeStruct((b, D_HEADS, 128, s), BF16),
        jax.ShapeDtypeStruct((b, 1, s, D_HEADS * 128), BF16),
        jax.ShapeDtypeStruct((b, D_HEADS, V_ROWS, s), BF16),
        jax.ShapeDtypeStruct((b, D_MIX, s), F32),
    ]
    out_specs = [
        chan(8, 128), tok((256,)), chan(A_HEADS, V_ROWS),
        chan(B_HEADS, 128), tok((128,)), chan(B_KV_HEADS, V_ROWS),
        chan(C_HEADS, 128), tok((256,)), chan(C_HEADS, V_ROWS),
        chan(D_HEADS, 128), tok((D_HEADS * 128,)), chan(D_HEADS, V_ROWS),
        pl.BlockSpec((1, D_MIX, tm), lambda i, j: (j, 0, i)),
    ]
    return pl.pallas_call(
        _proj_kernel,
        grid=(s // tm, b),
        in_specs=in_specs, out_specs=out_specs, out_shape=out_shape,
        compiler_params=pltpu.CompilerParams(
            dimension_semantics=("arbitrary", "arbitrary"), vmem_limit_bytes=VMEM_LIMIT),
        name="proj",
    )(h, norm_g.reshape(1, D_MODEL), wt, ws, ta, tb, td, sa, sb, sd,
      q_norm_g.reshape(D_Q_RANK, 1), kv_norm_g.reshape(D_KV_RANK, 1), kv_norm_g.reshape(1, D_KV_RANK),
      wqn, wqp, wkn, place, wvt)


def _flash_kernel(k_ref, q_ref, v_ref, o_ref, s_buf, p_buf, pv_buf, *, tk, nk, tq):
    total = (q_ref.shape[3] // tq) * nk

    def scores(t):
        b, j = divmod(t, nk)
        s = _dot(k_ref[0, 0, j * tk:(j + 1) * tk, :], q_ref[0, 0, :, b * tq:(b + 1) * tq])
        s_buf[j] = s
        return jnp.max(s, axis=0, keepdims=True)

    def softmax(t, m, cmax):
        j = t % nk
        m_new = jnp.maximum(m, cmax)
        p_buf[j] = jnp.exp2(s_buf[j] - m_new).astype(BF16)
        return m_new, jnp.exp2(m - m_new)

    def values(t, acc, alpha):
        j = t % nk
        pv_buf[...] = _dot(v_ref[0, 0, :, j * tk:(j + 1) * tk], p_buf[j])
        return acc * alpha + pv_buf[...]

    m_init = jnp.full((1, tq), NEG_INF, F32)
    acc_init = jnp.zeros((V_ROWS, tq), F32)
    m, acc = m_init, acc_init
    cmax = {0: scores(0)}
    alpha = {}
    for t in range(total + 1):
        if t + 1 < total:
            cmax[t + 1] = scores(t + 1)
        if t < total:
            if t % nk == 0:
                m = m_init
            m, alpha[t] = softmax(t, m, cmax.pop(t))
        if t >= 1:
            if (t - 1) % nk == 0:
                acc = acc_init
            acc = values(t - 1, acc, alpha.pop(t - 1))
            if t % nk == 0:
                b = t // nk - 1
                o_ref[0, 0, :, b * tq:(b + 1) * tq] = acc[0:64] / acc[64:65]


def _flash(k, qt, vt, k_group, v_group):
    b, g, dk, s = qt.shape
    tq, tk = min(TQ_DENSE, s), min(TK, s)
    nk = s // tk
    slots = nk
    assert slots * tk * tq * 6 <= FLASH_BUF_BYTES
    tqs = tq * math.gcd(s // tq, FLASH_QBLOCKS)
    return pl.pallas_call(
        functools.partial(_flash_kernel, tk=tk, nk=nk, tq=tq),
        grid=(b, g, s // tqs),
        scratch_shapes=[pltpu.VMEM((slots, tk, tq), F32), pltpu.VMEM((slots, tk, tq), BF16),
                        pltpu.VMEM((V_ROWS, tq), F32)],
        in_specs=[
            pl.BlockSpec((1, 1, s, dk), lambda bi, gi, i: (bi,) + k_group(gi)),
            pl.BlockSpec((1, 1, dk, tqs), lambda bi, gi, i: (bi, gi, 0, i)),
            pl.BlockSpec((1, 1, V_ROWS, s), lambda bi, gi, i: (bi, v_group(gi), 0, 0)),
        ],
        out_specs=pl.BlockSpec((1, 1, 64, tqs), lambda bi, gi, i: (bi, gi, 0, i)),
        out_shape=jax.ShapeDtypeStruct((b, g, 64, s), F32),
        compiler_params=pltpu.CompilerParams(
            dimension_semantics=("arbitrary", "arbitrary", "arbitrary"), vmem_limit_bytes=VMEM_LIMIT),
        name="flash",
    )(k, qt, vt)


def _banded_kernel(sink_ref, bias_ref, kb_ref, qb_ref, vb_ref, kc_ref, qc_ref, vc_ref, ob_ref, oc_ref, s_buf,
                   *, s_len, win_b, win_c):
    i = pl.program_id(1)
    tq = qb_ref.shape[3]
    q0 = i * tq
    start_b = pl.multiple_of(jnp.clip(q0 - B_WINDOW, 0, s_len - win_b), 128)
    start_c = pl.multiple_of(jnp.clip(q0 - (C_MAX_ROWS // 2) * GRID_W, 0, s_len - win_c), 256)
    kb = kb_ref[0, 0, pl.ds(start_b, win_b), :]
    kpos = start_b + lax.broadcasted_iota(jnp.int32, (win_b, tq), 0)
    qpos = q0 + lax.broadcasted_iota(jnp.int32, (win_b, tq), 1)
    in_band = jnp.abs(kpos - qpos) <= B_WINDOW
    sc_b = [_dot(kb, qb_ref[0, h]) for h in range(B_HEADS)]
    for h in range(C_HEADS):
        k = kc_ref[0, 0, pl.ds(start_c, win_c), 128 * (h // 2):128 * (h // 2) + 128]
        s_buf[h] = _dot(k, qc_ref[0, h])
    for h in range(B_HEADS):
        s = jnp.where(in_band, sc_b[h], NEG_INF)
        sink = sink_ref[h] * LOG2E
        m = jnp.maximum(jnp.max(s, axis=0, keepdims=True), sink)
        p = jnp.exp2(s - m).astype(BF16)
        acc = _dot(vb_ref[0, h // (B_HEADS // B_KV_HEADS), :, pl.ds(start_b, win_b)], p)
        ob_ref[0, h] = acc[0:64] / (acc[64:65] + jnp.exp2(sink - m))
        s = s_buf[h] + bias_ref[0, h]
        m = jnp.max(s, axis=0, keepdims=True)
        p = jnp.exp2(s - m).astype(BF16)
        acc = _dot(vc_ref[0, h, :, pl.ds(start_c, win_c)], p)
        oc_ref[0, h] = acc[0:64] / acc[64:65]


def _nbr_bias(rpb, rows):
    col = np.arange(GRID_W)
    cs = np.clip(col - C_WIN_COLS // 2, 0, GRID_W - C_WIN_COLS)
    col_ok = (col[:, None] >= cs[None, :]) & (col[:, None] < cs[None, :] + C_WIN_COLS)
    dc = np.clip(col[:, None] - col[None, :], -(C_WIN_COLS - 1), C_WIN_COLS - 1) + (C_WIN_COLS - 1)
    rp = rpb.astype(F32)
    by_cols = jnp.zeros(rp.shape[:2] + dc.shape, F32)
    for b in range(rp.shape[2]):
        by_cols = by_cols + jnp.where((dc == b)[None, None], rp[:, :, b][:, :, None, None], 0.0)
    kr = min(C_MAX_ROWS, rows)
    out = []
    for r0 in (0, C_QROWS, rows - C_QROWS):
        rk0 = min(max(r0 - C_MAX_ROWS // 2, 0), rows - C_KROWS)
        rq = r0 + np.arange(C_QROWS)
        rk = rk0 + np.arange(C_KROWS)
        rs = np.clip(rq - kr // 2, 0, rows - kr)
        row_ok = (rk[:, None] >= rs[None, :]) & (rk[:, None] < rs[None, :] + kr)
        dr = np.clip(rk[:, None] - rq[None, :] + (C_MAX_ROWS - 1), 0, 2 * C_MAX_ROWS - 2)
        blocks = jnp.stack([jnp.stack([by_cols[:, dr[a, c]] for c in range(C_QROWS)], axis=2)
                            for a in range(C_KROWS)], axis=1)
        ok = row_ok[:, None, :, None] & col_ok[None, :, None, :]
        bias = jnp.where(ok[None], blocks * LOG2E, NEG_INF)
        out.append(bias.reshape(rp.shape[0], C_KROWS * GRID_W, C_QROWS * GRID_W))
    return jnp.stack(out)


def _banded(kb, qb, vb, sink, kc, qc, vc, rpb):
    b, g, dk, s = qb.shape
    tq = C_QROWS * GRID_W
    win_b = tq + 2 * B_WINDOW
    win_c = C_KROWS * GRID_W
    nq = s // tq
    bias = _nbr_bias(rpb, s // GRID_W)

    def variant(i):
        return jnp.where(i == 0, 0, jnp.where(i == nq - 1, 2, 1))

    qspec = pl.BlockSpec((1, g, dk, tq), lambda bi, i: (bi, 0, 0, i))
    ospec = pl.BlockSpec((1, g, 64, tq), lambda bi, i: (bi, 0, 0, i))
    return pl.pallas_call(
        functools.partial(_banded_kernel, s_len=s, win_b=win_b, win_c=win_c),
        grid=(b, nq),
        scratch_shapes=[pltpu.VMEM((g, win_c, tq), F32)],
        in_specs=[
            pl.BlockSpec(memory_space=pltpu.SMEM),
            pl.BlockSpec((1, g, win_c, tq), lambda bi, i: (variant(i), 0, 0, 0)),
            pl.BlockSpec((1, 1, s, kb.shape[3]), lambda bi, i: (bi, 0, 0, 0)), qspec,
            pl.BlockSpec((1, B_KV_HEADS, V_ROWS, s), lambda bi, i: (bi, 0, 0, 0)),
            pl.BlockSpec((1, 1, s, kc.shape[3]), lambda bi, i: (bi, 0, 0, 0)), qspec,
            pl.BlockSpec((1, g, V_ROWS, s), lambda bi, i: (bi, 0, 0, 0)),
        ],
        out_specs=[ospec, ospec],
        out_shape=[jax.ShapeDtypeStruct((b, g, 64, s), F32)] * 2,
        compiler_params=pltpu.CompilerParams(
            dimension_semantics=("arbitrary", "arbitrary"), vmem_limit_bytes=VMEM_LIMIT),
        name="banded",
    )(sink.astype(F32), bias, kb, qb, vb, kc, qc, vc)


def _out_kernel(oa_ref, ob_ref, oc_ref, od_ref, sg_ref, h_ref, p_ref, lam_ref, sub_ref,
                wo_ref, pg_ref, wg_ref, wp_ref, fg_ref, y_ref, *, lam_init, final):
    ple = _dot(p_ref[0, 0].astype(BF16), wp_ref[...])
    lv = lam_ref[...]
    lam = (jnp.exp(jnp.sum(lv[0:1] * lv[1:2], axis=-1, keepdims=True))
           - jnp.exp(jnp.sum(lv[2:3] * lv[3:4], axis=-1, keepdims=True)) + lam_init)
    parts = []
    for h in range(A_HEADS):
        o = oa_ref[0, 2 * h] - lam * oa_ref[0, 2 * h + 1]
        on = (o * lax.rsqrt(jnp.mean(o * o, axis=0, keepdims=True) + EPS)) * sub_ref[...]
        parts.append(on * (1.0 - lam_init))
    for ref in (ob_ref, oc_ref, od_ref):
        for h in range(4):
            parts.append(ref[0, h])
    mix = (jnp.concatenate(parts, axis=0) * sg_ref[0]).astype(BF16)
    h1 = h_ref[0] + _tn(mix, wo_ref[...])
    hn = ((h1 * lax.rsqrt(jnp.mean(h1 * h1, axis=-1, keepdims=True) + EPS)) * pg_ref[...]).astype(BF16)
    gate = jax.nn.sigmoid(_dot(hn, wg_ref[...]))
    h2 = h1 + gate * ple
    if final:
        h2 = (h2 * lax.rsqrt(jnp.mean(h2 * h2, axis=-1, keepdims=True) + EPS)) * fg_ref[...]
    y_ref[0] = h2


def _out(oa, ob, oc, od, sg, h, p, layer, a_lambda, subln_g, w_out, ple_norm_g, w_ple_gate, w_ple_proj,
         final_norm_g, lam_init, final):
    b, s, _ = h.shape
    tm = min(TM, s)

    def heads(n):
        return pl.BlockSpec((1, n, 64, tm), lambda bi, i: (bi, 0, 0, i))

    return pl.pallas_call(
        functools.partial(_out_kernel, lam_init=lam_init, final=final),
        grid=(b, s // tm),
        in_specs=[
            heads(8), heads(4), heads(4), heads(4),
            pl.BlockSpec((1, D_MIX, tm), lambda bi, i: (bi, 0, i)),
            pl.BlockSpec((1, tm, D_MODEL), lambda bi, i: (bi, i, 0)),
            pl.BlockSpec((1, 1, tm, D_PLE), lambda bi, i: (layer, bi, i, 0)),
            _const_spec((4, A_QK_DIM)), _const_spec((A_V_DIM, 1)),
            _const_spec((D_MIX, D_MODEL)), _const_spec((1, D_MODEL)),
            _const_spec((D_MODEL, D_MODEL)), _const_spec((D_PLE, D_MODEL)), _const_spec((1, D_MODEL)),
        ],
        out_specs=pl.BlockSpec((1, tm, D_MODEL), lambda bi, i: (bi, i, 0)),
        out_shape=jax.ShapeDtypeStruct((b, s, D_MODEL), F32),
        compiler_params=pltpu.CompilerParams(
            dimension_semantics=("arbitrary", "arbitrary"), vmem_limit_bytes=VMEM_LIMIT),
        name="out",
    )(oa, ob, oc, od, sg, h, p, a_lambda.astype(F32), subln_g.reshape(A_V_DIM, 1).astype(F32),
      w_out.astype(BF16), ple_norm_g.reshape(1, D_MODEL), w_ple_gate.astype(BF16), w_ple_proj.astype(BF16),
      final_norm_g.reshape(1, D_MODEL))


def _trunk(x, p, tabs, preps, norm_g, a_lambda, a_subln_g, b_sink, c_rpb, d_q_norm_g, d_kv_norm_g,
           w_out, ple_norm_g, w_ple_gate, w_ple_proj, final_norm_g):
    depth = len(preps)
    s = x.shape[1]
    ta, tb, td, sa, sb, sd = tabs
    tabs = (ta[:, :, :s], tb[:, :, :s], td[:, :, :s], sa[:, :s], sb[:, :s], sd[:, :s])
    h = x
    for i in range(depth):
        (qa, ka, va, qb, kb, vb, qc, kc, vc, qd, kd, vd, sg) = _proj(
            h, norm_g[i], preps[i], tabs, d_q_norm_g[i], d_kv_norm_g[i])
        oa = _flash(ka, qa, va, lambda g: (0, 0, g // 4), lambda g: g // 2)
        od = _flash(kd, qd, vd, lambda g: (0, 0, g), lambda g: g)
        ob, oc = _banded(kb, qb, vb, b_sink[i], kc, qc, vc, c_rpb[i])
        lam_init = 0.8 - 0.6 * math.exp(-0.3 * i)
        h = _out(oa, ob, oc, od, sg, h, p, i, a_lambda[i], a_subln_g[i], w_out[i], ple_norm_g[i],
                 w_ple_gate[i], w_ple_proj[i], final_norm_g, lam_init, i == depth - 1)
    return h


def kernel(x_prompt, x_sample, p_prompt, p_sample, norm_g, w_in, a_lambda, a_subln_g, b_sink, c_rpb,
           d_q_norm_g, d_kv_norm_g, d_w_uq, d_w_ukv, w_out, ple_norm_g, w_ple_gate, w_ple_proj,
           final_norm_g):
    tabs = _rope_tables(max(x_prompt.shape[1], x_sample.shape[1]))
    preps = [_prep_layer(w_in[i], d_w_uq[i], d_w_ukv[i]) for i in range(w_in.shape[0])]
    weights = (norm_g, a_lambda, a_subln_g, b_sink, c_rpb, d_q_norm_g, d_kv_norm_g,
               w_out, ple_norm_g, w_ple_gate, w_ple_proj, final_norm_g)
    return (_trunk(x_prompt, p_prompt, tabs, preps, *weights), _trunk(x_sample, p_sample, tabs, preps, *weights))
```

```python
import functools
import math

import jax
import jax.numpy as jnp
import numpy as np
from jax import lax
from jax.experimental import pallas as pl
from jax.experimental.pallas import tpu as pltpu

F32 = jnp.float32
BF16 = jnp.bfloat16

D_MODEL = 1024
D_MIX = 1024
D_PLE = 256
GRID_W = 64
ROPE_THETA = 10000.0
EPS = 1e-6
NEG_INF = -1e30

A_HEADS, A_QK_DIM, A_V_DIM = 4, 32, 64
B_HEADS, B_KV_HEADS, B_HEAD_DIM, B_WINDOW = 4, 2, 64, 128
C_HEADS, C_HEAD_DIM, C_MAX_ROWS, C_WIN_COLS = 4, 64, 8, 16
D_HEADS, D_Q_RANK, D_KV_RANK, D_NOPE, D_ROPE, D_V = 4, 256, 128, 64, 32, 64

IN_SIZES = (256, 256, 256, 256, 128, 128, 256, 256, 256, D_Q_RANK, D_KV_RANK, D_ROPE, D_MIX)

V_ROWS = 80
VMEM_LIMIT = 56 * 1024 * 1024

TM = 512
TQ_DENSE = 512
LOG2E = math.log2(math.e)
D_SCALE = (D_NOPE + D_ROPE) ** -0.5 * LOG2E
TK = 256
FLASH_QBLOCKS = 8
FLASH_BUF_BYTES = 24 * 1024 * 1024
C_QROWS = 4
C_KROWS = 12


def _nt(a, b):
    return lax.dot_general(a, b, (((1,), (1,)), ((), ())), preferred_element_type=F32)


def _tn(a, b):
    return lax.dot_general(a, b, (((0,), (0,)), ((), ())), preferred_element_type=F32)


def _dot(a, b):
    return jnp.dot(a, b, preferred_element_type=F32)


T_AQ, T_BQ, T_CQ, T_DCQ, T_DCKV, T_AV, T_BV, T_CV, T_GATE, T_END = (
    0, 256, 512, 768, 1024, 1152, 1408, 1536, 1792, 2816)
S_AK, S_BK, S_CK, S_DCKV, S_KPE, S_END = (0, 256, 384, 640, 768, 896)


def _swap_row_halves(x, d):
    h = d // 2
    return jnp.concatenate([x[r + o:r + o + h] for r in range(0, x.shape[0], d) for o in (h, 0)], axis=0)


def _swap_lane_halves(x, d):
    n = x.shape[1]
    lane = lax.broadcasted_iota(jnp.int32, x.shape, 1)
    return jnp.where(lane % d < d // 2, pltpu.roll(x, n - d // 2, 1), pltpu.roll(x, d // 2, 1))


def _proj_kernel(h_ref, g_ref, wt_ref, ws_ref, ta_ref, tb_ref, td_ref, sa_ref, sb_ref, sd_ref,
                 qg_ref, kvgc_ref, kvgr_ref, wqn_ref, wqp_ref, wkn_ref, place_ref, wvt_ref,
                 qa_ref, ka_ref, va_ref, qb_ref, kb_ref, vb_ref, qc_ref, kc_ref, vc_ref,
                 qd_ref, kd_ref, vd_ref, sg_ref):
    x = h_ref[0]
    tm = x.shape[0]
    ms = jnp.mean(x * x, axis=-1, keepdims=True)
    hn = ((x * lax.rsqrt(ms + EPS)) * g_ref[...]).astype(BF16)

    zs = _dot(hn, ws_ref[...])
    zak, zbk, zpe = zs[:, S_AK:S_BK], zs[:, S_BK:S_CK], zs[:, S_KPE:S_END]
    ka_ref[0, 0] = (zak * sa_ref[0] + _swap_lane_halves(zak, A_QK_DIM) * sa_ref[1]).astype(BF16)
    kb_ref[0, 0] = (zbk * sb_ref[0] + _swap_lane_halves(zbk, B_HEAD_DIM) * sb_ref[1]).astype(BF16)
    kc_ref[0, 0] = zs[:, S_CK:S_DCKV].astype(BF16)
    ck = zs[:, S_DCKV:S_KPE]
    ckn = ((ck * lax.rsqrt(jnp.mean(ck * ck, axis=-1, keepdims=True) + EPS)) * kvgr_ref[...]).astype(BF16)
    kpe = (zpe * sd_ref[0] + _swap_lane_halves(zpe, D_ROPE) * sd_ref[1]).astype(BF16)
    kd_ref[0, 0] = (_dot(ckn, wkn_ref[...]) + _dot(kpe, place_ref[...])).astype(BF16)

    zq = _nt(wt_ref[T_AQ:T_AV, :], hn)
    zv = _nt(wt_ref[T_AV:T_GATE, :], hn)
    gt = _nt(wt_ref[T_GATE:T_END, :], hn)
    sg_ref[0] = gt * jax.nn.sigmoid(gt)

    ones_rows = (lax.broadcasted_iota(jnp.int32, (V_ROWS - 64, tm), 0) == 0).astype(BF16)

    def store_values(ref, v, heads):
        for h in range(heads):
            ref[0, h, 0:64, :] = v[64 * h:64 * h + 64].astype(BF16)
            ref[0, h, 64:V_ROWS, :] = ones_rows

    za = zq[T_AQ:T_BQ]
    qa = za * ta_ref[0] + _swap_row_halves(za, A_QK_DIM) * ta_ref[1]
    for g in range(2 * A_HEADS):
        r = 32 * (g % 4)
        qa_ref[0, g] = jnp.zeros((128, tm), BF16)
        qa_ref[0, g, r:r + 32, :] = qa[32 * g:32 * g + 32].astype(BF16)
    store_values(va_ref, zv[0:T_BV - T_AV], A_HEADS)

    zb = zq[T_BQ:T_CQ]
    qb = zb * tb_ref[0] + _swap_row_halves(zb, B_HEAD_DIM) * tb_ref[1]
    for h in range(B_HEADS):
        r = 64 * (h // 2)
        qb_ref[0, h] = jnp.zeros((128, tm), BF16)
        qb_ref[0, h, r:r + 64, :] = qb[64 * h:64 * h + 64].astype(BF16)
    store_values(vb_ref, zv[T_BV - T_AV:T_CV - T_AV], B_KV_HEADS)

    qc = zq[T_CQ:T_DCQ] * (C_HEAD_DIM ** -0.5 * LOG2E)
    for h in range(C_HEADS):
        r = 64 * (h % 2)
        qc_ref[0, h] = jnp.zeros((128, tm), BF16)
        qc_ref[0, h, r:r + 64, :] = qc[64 * h:64 * h + 64].astype(BF16)
    store_values(vc_ref, zv[T_CV - T_AV:T_GATE - T_AV], C_HEADS)

    cq = zq[T_DCQ:T_DCKV]
    cqn = ((cq * lax.rsqrt(jnp.mean(cq * cq, axis=0, keepdims=True) + EPS)) * qg_ref[...]).astype(BF16)
    for h in range(D_HEADS):
        qn = _dot(wqn_ref[h], cqn) * D_SCALE
        qp = _dot(wqp_ref[h], cqn)
        qpe = qp * td_ref[0] + _swap_row_halves(qp, D_ROPE) * td_ref[1]
        qd_ref[0, h, 0:64, :] = qn.astype(BF16)
        qd_ref[0, h, 64:96, :] = qpe.astype(BF16)
        qd_ref[0, h, 96:128, :] = jnp.zeros((32, tm), BF16)
    ckt = zq[T_DCKV:T_AV]
    cktn = ((ckt * lax.rsqrt(jnp.mean(ckt * ckt, axis=0, keepdims=True) + EPS)) * kvgc_ref[...]).astype(BF16)
    store_values(vd_ref, _dot(wvt_ref[...], cktn), D_HEADS)


def _rope_cs(s, dim):
    inv = 1.0 / (ROPE_THETA ** (jnp.arange(0, dim, 2, dtype=F32) / dim))
    ang = jnp.arange(s, dtype=F32)[:, None] * inv[None, :]
    cos, sin = jnp.cos(ang), jnp.sin(ang)
    return jnp.concatenate([cos, cos], -1), jnp.concatenate([-sin, sin], -1)


def _rope_tables(s):
    c32, s32 = _rope_cs(s, A_QK_DIM)
    c64, s64 = _rope_cs(s, B_HEAD_DIM)
    a_scale, b_scale, d_scale = A_QK_DIM ** -0.5 * LOG2E, B_HEAD_DIM ** -0.5 * LOG2E, D_SCALE
    ta = jnp.stack([jnp.tile(c32, (1, 8)).T, jnp.tile(s32, (1, 8)).T]) * a_scale
    tb = jnp.stack([jnp.tile(c64, (1, 4)).T, jnp.tile(s64, (1, 4)).T]) * b_scale
    td = jnp.stack([c32.T, s32.T]) * d_scale
    sa = jnp.stack([jnp.tile(c32, (1, 8)), jnp.tile(s32, (1, 8))])
    sb = jnp.stack([jnp.tile(c64, (1, 2)), jnp.tile(s64, (1, 2))])
    pad = jnp.zeros((s, 128 - D_ROPE), F32)
    sd = jnp.stack([jnp.concatenate([c32, pad], -1), jnp.concatenate([s32, pad], -1)])
    return ta, tb, td, sa, sb, sd


def _prep_layer(w_in, d_w_uq, d_w_ukv):
    pts = [0]
    for n in IN_SIZES:
        pts.append(pts[-1] + n)
    (aq, ak, av, bq, bk, bv, cq, ck, cv, dcq, dckv, dkr, gate) = [w_in[:, pts[i]:pts[i + 1]] for i in range(13)]
    wt = jnp.concatenate([aq, bq, cq, dcq, dckv, av, bv, cv, gate], axis=1).T.astype(BF16)
    zpad = jnp.zeros((D_MODEL, 128 - D_ROPE), F32)
    ws = jnp.concatenate([ak, bk, ck, dckv, dkr, zpad], axis=1).astype(BF16)
    uq = d_w_uq.reshape(D_Q_RANK, D_HEADS, D_NOPE + D_ROPE)
    uq_n, uq_p = uq[:, :, :D_NOPE], uq[:, :, D_NOPE:]
    wqn = jnp.transpose(uq_n, (1, 2, 0))
    wqp = jnp.transpose(uq_p, (1, 2, 0))
    ukv = d_w_ukv.reshape(D_KV_RANK, D_HEADS, D_NOPE + D_V)
    wkn = jnp.pad(ukv[:, :, :D_NOPE], ((0, 0), (0, 0), (0, 128 - D_NOPE))).reshape(D_KV_RANK, D_HEADS * 128)
    place = np.zeros((128, D_HEADS, 128), np.float32)
    place[np.arange(D_ROPE), :, D_NOPE + np.arange(D_ROPE)] = 1.0
    place = jnp.asarray(place.reshape(128, D_HEADS * 128), BF16)
    wvt = ukv[:, :, D_NOPE:].reshape(D_KV_RANK, D_HEADS * D_V).T
    return (wt, ws, wqn.astype(BF16), wqp.astype(BF16), wkn.astype(BF16), place, wvt.astype(BF16))


def _const_spec(shape):
    nd = len(shape)
    return pl.BlockSpec(shape, lambda *_: (0,) * nd)


def _proj(h, norm_g, prep, tabs, q_norm_g, kv_norm_g):
    b, s, _ = h.shape
    tm = min(TM, s)
    wt, ws, wqn, wqp, wkn, place, wvt = prep
    ta, tb, td, sa, sb, sd = tabs

    def tok(shape_tail):
        return pl.BlockSpec((1, 1, tm) + shape_tail, lambda i, j: (j, 0, i, 0))

    def chan(groups, rows):
        return pl.BlockSpec((1, groups, rows, tm), lambda i, j: (j, 0, 0, i))

    in_specs = [
        pl.BlockSpec((1, tm, D_MODEL), lambda i, j: (j, i, 0)),
        _const_spec((1, D_MODEL)), _const_spec(wt.shape), _const_spec(ws.shape),
        pl.BlockSpec((2, 256, tm), lambda i, j: (0, 0, i)),
        pl.BlockSpec((2, 256, tm), lambda i, j: (0, 0, i)),
        pl.BlockSpec((2, D_ROPE, tm), lambda i, j: (0, 0, i)),
        pl.BlockSpec((2, tm, 256), lambda i, j: (0, i, 0)),
        pl.BlockSpec((2, tm, 128), lambda i, j: (0, i, 0)),
        pl.BlockSpec((2, tm, 128), lambda i, j: (0, i, 0)),
        _const_spec((D_Q_RANK, 1)), _const_spec((D_KV_RANK, 1)), _const_spec((1, D_KV_RANK)),
        _const_spec(wqn.shape), _const_spec(wqp.shape),
        _const_spec(wkn.shape), _const_spec(place.shape), _const_spec(wvt.shape),
    ]
    out_shape = [
        jax.ShapeDtypeStruct((b, 8, 128, s), BF16),
        jax.ShapeDtypeStruct((b, 1, s, 256), BF16),
        jax.ShapeDtypeStruct((b, A_HEADS, V_ROWS, s), BF16),
        jax.ShapeDtypeStruct((b, B_HEADS, 128, s), BF16),
        jax.ShapeDtypeStruct((b, 1, s, 128), BF16),
        jax.ShapeDtypeStruct((b, B_KV_HEADS, V_ROWS, s), BF16),
        jax.ShapeDtypeStruct((b, C_HEADS, 128, s), BF16),
        jax.ShapeDtypeStruct((b, 1, s, 256), BF16),
        jax.ShapeDtypeStruct((b, C_HEADS, V_ROWS, s), BF16),
        jax.ShapeDtypeStruct((b, D_HEADS, 128, s), BF16),
        jax.ShapeDtypeStruct((b, 1, s, D_HEADS * 128), BF16),
        jax.ShapeDtypeStruct((b, D_HEADS, V_ROWS, s), BF16),
        jax.ShapeDtypeStruct((b, D_MIX, s), F32),
    ]
    out_specs = [
        chan(8, 128), tok((256,)), chan(A_HEADS, V_ROWS),
        chan(B_HEADS, 128), tok((128,)), chan(B_KV_HEADS, V_ROWS),
        chan(C_HEADS, 128), tok((256,)), chan(C_HEADS, V_ROWS),
        chan(D_HEADS, 128), tok((D_HEADS * 128,)), chan(D_HEADS, V_ROWS),
        pl.BlockSpec((1, D_MIX, tm), lambda i, j: (j, 0, i)),
    ]
    return pl.pallas_call(
        _proj_kernel,
        grid=(s // tm, b),
        in_specs=in_specs, out_specs=out_specs, out_shape=out_shape,
        compiler_params=pltpu.CompilerParams(
            dimension_semantics=("arbitrary", "arbitrary"), vmem_limit_bytes=VMEM_LIMIT),
        name="proj",
    )(h, norm_g.reshape(1, D_MODEL), wt, ws, ta, tb, td, sa, sb, sd,
      q_norm_g.reshape(D_Q_RANK, 1), kv_norm_g.reshape(D_KV_RANK, 1), kv_norm_g.reshape(1, D_KV_RANK),
      wqn, wqp, wkn, place, wvt)


def _flash_kernel(k_ref, q_ref, v_ref, o_ref, s_buf, p_buf, pv_buf, *, tk, nk, tq):
    total = (q_ref.shape[3] // tq) * nk

    def scores(t):
        b, j = divmod(t, nk)
        s = _dot(k_ref[0, 0, j * tk:(j + 1) * tk, :], q_ref[0, 0, :, b * tq:(b + 1) * tq])
        s_buf[j] = s
        return jnp.max(s, axis=0, keepdims=True)

    def softmax(t, m, cmax):
        j = t % nk
        m_new = jnp.maximum(m, cmax)
        p_buf[j] = jnp.exp2(s_buf[j] - m_new).astype(BF16)
        return m_new, jnp.exp2(m - m_new)

    def values(t, acc, alpha):
        j = t % nk
        pv_buf[...] = _dot(v_ref[0, 0, :, j * tk:(j + 1) * tk], p_buf[j])
        return acc * alpha + pv_buf[...]

    m_init = jnp.full((1, tq), NEG_INF, F32)
    acc_init = jnp.zeros((V_ROWS, tq), F32)
    m, acc = m_init, acc_init
    cmax = {0: scores(0)}
    alpha = {}
    for t in range(total + 1):
        if t + 1 < total:
            cmax[t + 1] = scores(t + 1)
        if t < total:
            if t % nk == 0:
                m = m_init
            m, alpha[t] = softmax(t, m, cmax.pop(t))
        if t >= 1:
            if (t - 1) % nk == 0:
                acc = acc_init
            acc = values(t - 1, acc, alpha.pop(t - 1))
            if t % nk == 0:
                b = t // nk - 1
                o_ref[0, 0, :, b * tq:(b + 1) * tq] = acc[0:64] / acc[64:65]


def _flash(k, qt, vt, k_group, v_group):
    b, g, dk, s = qt.shape
    tq, tk = min(TQ_DENSE, s), min(TK, s)
    nk = s // tk
    slots = nk
    assert slots * tk * tq * 6 <= FLASH_BUF_BYTES
    tqs = tq * math.gcd(s // tq, FLASH_QBLOCKS)
    return pl.pallas_call(
        functools.partial(_flash_kernel, tk=tk, nk=nk, tq=tq),
        grid=(b, g, s // tqs),
        scratch_shapes=[pltpu.VMEM((slots, tk, tq), F32), pltpu.VMEM((slots, tk, tq), BF16),
                        pltpu.VMEM((V_ROWS, tq), F32)],
        in_specs=[
            pl.BlockSpec((1, 1, s, dk), lambda bi, gi, i: (bi,) + k_group(gi)),
            pl.BlockSpec((1, 1, dk, tqs), lambda bi, gi, i: (bi, gi, 0, i)),
            pl.BlockSpec((1, 1, V_ROWS, s), lambda bi, gi, i: (bi, v_group(gi), 0, 0)),
        ],
        out_specs=pl.BlockSpec((1, 1, 64, tqs), lambda bi, gi, i: (bi, gi, 0, i)),
        out_shape=jax.ShapeDtypeStruct((b, g, 64, s), F32),
        compiler_params=pltpu.CompilerParams(
            dimension_semantics=("arbitrary", "arbitrary", "arbitrary"), vmem_limit_bytes=VMEM_LIMIT),
        name="flash",
    )(k, qt, vt)


def _banded_kernel(sink_ref, bias_ref, kb_ref, qb_ref, vb_ref, kc_ref, qc_ref, vc_ref, ob_ref, oc_ref, s_buf,
                   *, s_len, win_b, win_c):
    i = pl.program_id(1)
    tq = qb_ref.shape[3]
    q0 = i * tq
    start_b = pl.multiple_of(jnp.clip(q0 - B_WINDOW, 0, s_len - win_b), 128)
    start_c = pl.multiple_of(jnp.clip(q0 - (C_MAX_ROWS // 2) * GRID_W, 0, s_len - win_c), 256)
    kb = kb_ref[0, 0, pl.ds(start_b, win_b), :]
    kpos = start_b + lax.broadcasted_iota(jnp.int32, (win_b, tq), 0)
    qpos = q0 + lax.broadcasted_iota(jnp.int32, (win_b, tq), 1)
    in_band = jnp.abs(kpos - qpos) <= B_WINDOW
    sc_b = [_dot(kb, qb_ref[0, h]) for h in range(B_HEADS)]
    for h in range(C_HEADS):
        k = kc_ref[0, 0, pl.ds(start_c, win_c), 128 * (h // 2):128 * (h // 2) + 128]
        s_buf[h] = _dot(k, qc_ref[0, h])
    for h in range(B_HEADS):
        s = jnp.where(in_band, sc_b[h], NEG_INF)
        sink = sink_ref[h] * LOG2E
        m = jnp.maximum(jnp.max(s, axis=0, keepdims=True), sink)
        p = jnp.exp2(s - m).astype(BF16)
        acc = _dot(vb_ref[0, h // (B_HEADS // B_KV_HEADS), :, pl.ds(start_b, win_b)], p)
        ob_ref[0, h] = acc[0:64] / (acc[64:65] + jnp.exp2(sink - m))
        s = s_buf[h] + bias_ref[0, h]
        m = jnp.max(s, axis=0, keepdims=True)
        p = jnp.exp2(s - m).astype(BF16)
        acc = _dot(vc_ref[0, h, :, pl.ds(start_c, win_c)], p)
        oc_ref[0, h] = acc[0:64] / acc[64:65]


def _nbr_bias(rpb, rows):
    col = np.arange(GRID_W)
    cs = np.clip(col - C_WIN_COLS // 2, 0, GRID_W - C_WIN_COLS)
    col_ok = (col[:, None] >= cs[None, :]) & (col[:, None] < cs[None, :] + C_WIN_COLS)
    dc = np.clip(col[:, None] - col[None, :], -(C_WIN_COLS - 1), C_WIN_COLS - 1) + (C_WIN_COLS - 1)
    rp = rpb.astype(F32)
    by_cols = jnp.zeros(rp.shape[:2] + dc.shape, F32)
    for b in range(rp.shape[2]):
        by_cols = by_cols + jnp.where((dc == b)[None, None], rp[:, :, b][:, :, None, None], 0.0)
    kr = min(C_MAX_ROWS, rows)
    out = []
    for r0 in (0, C_QROWS, rows - C_QROWS):
        rk0 = min(max(r0 - C_MAX_ROWS // 2, 0), rows - C_KROWS)
        rq = r0 + np.arange(C_QROWS)
        rk = rk0 + np.arange(C_KROWS)
        rs = np.clip(rq - kr // 2, 0, rows - kr)
        row_ok = (rk[:, None] >= rs[None, :]) & (rk[:, None] < rs[None, :] + kr)
        dr = np.clip(rk[:, None] - rq[None, :] + (C_MAX_ROWS - 1), 0, 2 * C_MAX_ROWS - 2)
        blocks = jnp.stack([jnp.stack([by_cols[:, dr[a, c]] for c in range(C_QROWS)], axis=2)
                            for a in range(C_KROWS)], axis=1)
        ok = row_ok[:, None, :, None] & col_ok[None, :, None, :]
        bias = jnp.where(ok[None], blocks * LOG2E, NEG_INF)
        out.append(bias.reshape(rp.shape[0], C_KROWS * GRID_W, C_QROWS * GRID_W))
    return jnp.stack(out)


def _banded(kb, qb, vb, sink, kc, qc, vc, rpb):
    b, g, dk, s = qb.shape
    tq = C_QROWS * GRID_W
    win_b = tq + 2 * B_WINDOW
    win_c = C_KROWS * GRID_W
    nq = s // tq
    bias = _nbr_bias(rpb, s // GRID_W)

    def variant(i):
        return jnp.where(i == 0, 0, jnp.where(i == nq - 1, 2, 1))

    qspec = pl.BlockSpec((1, g, dk, tq), lambda bi, i: (bi, 0, 0, i))
    ospec = pl.BlockSpec((1, g, 64, tq), lambda bi, i: (bi, 0, 0, i))
    return pl.pallas_call(
        functools.partial(_banded_kernel, s_len=s, win_b=win_b, win_c=win_c),
        grid=(b, nq),
        scratch_shapes=[pltpu.VMEM((g, win_c, tq), F32)],
        in_specs=[
            pl.BlockSpec(memory_space=pltpu.SMEM),
            pl.BlockSpec((1, g, win_c, tq), lambda bi, i: (variant(i), 0, 0, 0)),
            pl.BlockSpec((1, 1, s, kb.shape[3]), lambda bi, i: (bi, 0, 0, 0)), qspec,
            pl.BlockSpec((1, B_KV_HEADS, V_ROWS, s), lambda bi, i: (bi, 0, 0, 0)),
            pl.BlockSpec((1, 1, s, kc.shape[3]), lambda bi, i: (bi, 0, 0, 0)), qspec,
            pl.BlockSpec((1, g, V_ROWS, s), lambda bi, i: (bi, 0, 0, 0)),
        ],
        out_specs=[ospec, ospec],
        out_shape=[jax.ShapeDtypeStruct((b, g, 64, s), F32)] * 2,
        compiler_params=pltpu.CompilerParams(
            dimension_semantics=("arbitrary", "arbitrary"), vmem_limit_bytes=VMEM_LIMIT),
        name="banded",
    )(sink.astype(F32), bias, kb, qb, vb, kc, qc, vc)


def _out_kernel(oa_ref, ob_ref, oc_ref, od_ref, sg_ref, h_ref, p_ref, lam_ref, sub_ref,
                wo_ref, pg_ref, wg_ref, wp_ref, fg_ref, y_ref, *, lam_init, final):
    ple = _dot(p_ref[0, 0].astype(BF16), wp_ref[...])
    lv = lam_ref[...]
    lam = (jnp.exp(jnp.sum(lv[0:1] * lv[1:2], axis=-1, keepdims=True))
           - jnp.exp(jnp.sum(lv[2:3] * lv[3:4], axis=-1, keepdims=True)) + lam_init)
    parts = []
    for h in range(A_HEADS):
        o = oa_ref[0, 2 * h] - lam * oa_ref[0, 2 * h + 1]
        on = (o * lax.rsqrt(jnp.mean(o * o, axis=0, keepdims=True) + EPS)) * sub_ref[...]
        parts.append(on * (1.0 - lam_init))
    for ref in (ob_ref, oc_ref, od_ref):
        for h in range(4):
            parts.append(ref[0, h])
    mix = (jnp.concatenate(parts, axis=0) * sg_ref[0]).astype(BF16)
    h1 = h_ref[0] + _tn(mix, wo_ref[...])
    hn = ((h1 * lax.rsqrt(jnp.mean(h1 * h1, axis=-1, keepdims=True) + EPS)) * pg_ref[...]).astype(BF16)
    gate = jax.nn.sigmoid(_dot(hn, wg_ref[...]))
    h2 = h1 + gate * ple
    if final:
        h2 = (h2 * lax.rsqrt(jnp.mean(h2 * h2, axis=-1, keepdims=True) + EPS)) * fg_ref[...]
    y_ref[0] = h2


def _out(oa, ob, oc, od, sg, h, p, layer, a_lambda, subln_g, w_out, ple_norm_g, w_ple_gate, w_ple_proj,
         final_norm_g, lam_init, final):
    b, s, _ = h.shape
    tm = min(TM, s)

    def heads(n):
        return pl.BlockSpec((1, n, 64, tm), lambda bi, i: (bi, 0, 0, i))

    return pl.pallas_call(
        functools.partial(_out_kernel, lam_init=lam_init, final=final),
        grid=(b, s // tm),
        in_specs=[
            heads(8), heads(4), heads(4), heads(4),
            pl.BlockSpec((1, D_MIX, tm), lambda bi, i: (bi, 0, i)),
            pl.BlockSpec((1, tm, D_MODEL), lambda bi, i: (bi, i, 0)),
            pl.BlockSpec((1, 1, tm, D_PLE), lambda bi, i: (layer, bi, i, 0)),
            _const_spec((4, A_QK_DIM)), _const_spec((A_V_DIM, 1)),
            _const_spec((D_MIX, D_MODEL)), _const_spec((1, D_MODEL)),
            _const_spec((D_MODEL, D_MODEL)), _const_spec((D_PLE, D_MODEL)), _const_spec((1, D_MODEL)),
        ],
        out_specs=pl.BlockSpec((1, tm, D_MODEL), lambda bi, i: (bi, i, 0)),
        out_shape=jax.ShapeDtypeStruct((b, s, D_MODEL), F32),
        compiler_params=pltpu.CompilerParams(
            dimension_semantics=("arbitrary", "arbitrary"), vmem_limit_bytes=VMEM_LIMIT),
        name="out",
    )(oa, ob, oc, od, sg, h, p, a_lambda.astype(F32), subln_g.reshape(A_V_DIM, 1).astype(F32),
      w_out.astype(BF16), ple_norm_g.reshape(1, D_MODEL), w_ple_gate.astype(BF16), w_ple_proj.astype(BF16),
      final_norm_g.reshape(1, D_MODEL))


def _trunk(x, p, tabs, preps, norm_g, a_lambda, a_subln_g, b_sink, c_rpb, d_q_norm_g, d_kv_norm_g,
           w_out, ple_norm_g, w_ple_gate, w_ple_proj, final_norm_g):
    depth = len(preps)
    s = x.shape[1]
    ta, tb, td, sa, sb, sd = tabs
    tabs = (ta[:, :, :s], tb[:, :, :s], td[:, :, :s], sa[:, :s], sb[:, :s], sd[:, :s])
    h = x
    for i in range(depth):
        (qa, ka, va, qb, kb, vb, qc, kc, vc, qd, kd, vd, sg) = _proj(
            h, norm_g[i], preps[i], tabs, d_q_norm_g[i], d_kv_norm_g[i])
        oa = _flash(ka, qa, va, lambda g: (0, 0, g // 4), lambda g: g // 2)
        od = _flash(kd, qd, vd, lambda g: (0, 0, g), lambda g: g)
        ob, oc = _banded(kb, qb, vb, b_sink[i], kc, qc, vc, c_rpb[i])
        lam_init = 0.8 - 0.6 * math.exp(-0.3 * i)
        h = _out(oa, ob, oc, od, sg, h, p, i, a_lambda[i], a_subln_g[i], w_out[i], ple_norm_g[i],
                 w_ple_gate[i], w_ple_proj[i], final_norm_g, lam_init, i == depth - 1)
    return h


def kernel(x_prompt, x_sample, p_prompt, p_sample, norm_g, w_in, a_lambda, a_subln_g, b_sink, c_rpb,
           d_q_norm_g, d_kv_norm_g, d_w_uq, d_w_ukv, w_out, ple_norm_g, w_ple_gate, w_ple_proj,
           final_norm_g):
    tabs = _rope_tables(max(x_prompt.shape[1], x_sample.shape[1]))
    preps = [_prep_layer(w_in[i], d_w_uq[i], d_w_ukv[i]) for i in range(w_in.shape[0])]
    weights = (norm_g, a_lambda, a_subln_g, b_sink, c_rpb, d_q_norm_g, d_kv_norm_g,
               w_out, ple_norm_g, w_ple_gate, w_ple_proj, final_norm_g)
    return (_trunk(x_prompt, p_prompt, tabs, preps, *weights), _trunk(x_sample, p_sample, tabs, preps, *weights))
```

```python
import functools
import math

import jax
import jax.numpy as jnp
import numpy as np
from jax import lax
from jax.experimental import pallas as pl
from jax.experimental.pallas import tpu as pltpu

F32 = jnp.float32
BF16 = jnp.bfloat16

D_MODEL = 1024
D_MIX = 1024
D_PLE = 256
GRID_W = 64
ROPE_THETA = 10000.0
EPS = 1e-6
NEG_INF = -1e30

A_HEADS, A_QK_DIM, A_V_DIM = 4, 32, 64
B_HEADS, B_KV_HEADS, B_HEAD_DIM, B_WINDOW = 4, 2, 64, 128
C_HEADS, C_HEAD_DIM, C_MAX_ROWS, C_WIN_COLS = 4, 64, 8, 16
D_HEADS, D_Q_RANK, D_KV_RANK, D_NOPE, D_ROPE, D_V = 4, 256, 128, 64, 32, 64

IN_SIZES = (256, 256, 256, 256, 128, 128, 256, 256, 256, D_Q_RANK, D_KV_RANK, D_ROPE, D_MIX)

V_ROWS = 80
VMEM_LIMIT = 56 * 1024 * 1024

TM = 512
TQ_DENSE = 512
LOG2E = math.log2(math.e)
D_SCALE = (D_NOPE + D_ROPE) ** -0.5 * LOG2E
TK = 256
FLASH_QBLOCKS = 8
FLASH_BUF_BYTES = 24 * 1024 * 1024
C_QROWS = 4
C_KROWS = 12


def _nt(a, b):
    return lax.dot_general(a, b, (((1,), (1,)), ((), ())), preferred_element_type=F32)


def _tn(a, b):
    return lax.dot_general(a, b, (((0,), (0,)), ((), ())), preferred_element_type=F32)


def _dot(a, b):
    return jnp.dot(a, b, preferred_element_type=F32)


T_AQ, T_BQ, T_CQ, T_DCQ, T_DCKV, T_AV, T_BV, T_CV, T_GATE, T_END = (
    0, 256, 512, 768, 1024, 1152, 1408, 1536, 1792, 2816)
S_AK, S_BK, S_CK, S_DCKV, S_KPE, S_END = (0, 256, 384, 640, 768, 896)


def _swap_row_halves(x, d):
    h = d // 2
    return jnp.concatenate([x[r + o:r + o + h] for r in range(0, x.shape[0], d) for o in (h, 0)], axis=0)


def _swap_lane_halves(x, d):
    n = x.shape[1]
    lane = lax.broadcasted_iota(jnp.int32, x.shape, 1)
    return jnp.where(lane % d < d // 2, pltpu.roll(x, n - d // 2, 1), pltpu.roll(x, d // 2, 1))


def _proj_kernel(h_ref, g_ref, wt_ref, ws_ref, ta_ref, tb_ref, td_ref, sa_ref, sb_ref, sd_ref,
                 qg_ref, kvgc_ref, kvgr_ref, wqn_ref, wqp_ref, wkn_ref, place_ref, wvt_ref,
                 qa_ref, ka_ref, va_ref, qb_ref, kb_ref, vb_ref, qc_ref, kc_ref, vc_ref,
                 qd_ref, kd_ref, vd_ref, sg_ref):
    x = h_ref[0]
    tm = x.shape[0]
    ms = jnp.mean(x * x, axis=-1, keepdims=True)
    hn = ((x * lax.rsqrt(ms + EPS)) * g_ref[...]).astype(BF16)

    zs = _dot(hn, ws_ref[...])
    zak, zbk, zpe = zs[:, S_AK:S_BK], zs[:, S_BK:S_CK], zs[:, S_KPE:S_END]
    ka_ref[0, 0] = (zak * sa_ref[0] + _swap_lane_halves(zak, A_QK_DIM) * sa_ref[1]).astype(BF16)
    kb_ref[0, 0] = (zbk * sb_ref[0] + _swap_lane_halves(zbk, B_HEAD_DIM) * sb_ref[1]).astype(BF16)
    kc_ref[0, 0] = zs[:, S_CK:S_DCKV].astype(BF16)
    ck = zs[:, S_DCKV:S_KPE]
    ckn = ((ck * lax.rsqrt(jnp.mean(ck * ck, axis=-1, keepdims=True) + EPS)) * kvgr_ref[...]).astype(BF16)
    kpe = (zpe * sd_ref[0] + _swap_lane_halves(zpe, D_ROPE) * sd_ref[1]).astype(BF16)
    kd_ref[0, 0] = (_dot(ckn, wkn_ref[...]) + _dot(kpe, place_ref[...])).astype(BF16)

    zq = _nt(wt_ref[T_AQ:T_AV, :], hn)
    zv = _nt(wt_ref[T_AV:T_GATE, :], hn)
    gt = _nt(wt_ref[T_GATE:T_END, :], hn)
    sg_ref[0] = (gt * jax.nn.sigmoid(gt)).astype(BF16)

    ones_rows = (lax.broadcasted_iota(jnp.int32, (V_ROWS - 64, tm), 0) == 0).astype(BF16)

    def store_values(ref, v, heads):
        for h in range(heads):
            ref[0, h, 0:64, :] = v[64 * h:64 * h + 64].astype(BF16)
            ref[0, h, 64:V_ROWS, :] = ones_rows

    za = zq[T_AQ:T_BQ]
    qa = (za * jnp.tile(ta_ref[0], (2 * A_HEADS, 1))
          + _swap_row_halves(za, A_QK_DIM) * jnp.tile(ta_ref[1], (2 * A_HEADS, 1)))
    for g in range(2 * A_HEADS):
        r = 32 * (g % 4)
        qa_ref[0, g] = jnp.zeros((128, tm), BF16)
        qa_ref[0, g, r:r + 32, :] = qa[32 * g:32 * g + 32].astype(BF16)
    store_values(va_ref, zv[0:T_BV - T_AV], A_HEADS)

    zb = zq[T_BQ:T_CQ]
    qb = (zb * jnp.tile(tb_ref[0], (B_HEADS, 1))
          + _swap_row_halves(zb, B_HEAD_DIM) * jnp.tile(tb_ref[1], (B_HEADS, 1)))
    for h in range(B_HEADS):
        r = 64 * (h // 2)
        qb_ref[0, h] = jnp.zeros((128, tm), BF16)
        qb_ref[0, h, r:r + 64, :] = qb[64 * h:64 * h + 64].astype(BF16)
    store_values(vb_ref, zv[T_BV - T_AV:T_CV - T_AV], B_KV_HEADS)

    qc = zq[T_CQ:T_DCQ] * (C_HEAD_DIM ** -0.5 * LOG2E)
    for h in range(C_HEADS):
        r = 64 * (h % 2)
        qc_ref[0, h] = jnp.zeros((128, tm), BF16)
        qc_ref[0, h, r:r + 64, :] = qc[64 * h:64 * h + 64].astype(BF16)
    store_values(vc_ref, zv[T_CV - T_AV:T_GATE - T_AV], C_HEADS)

    cq = zq[T_DCQ:T_DCKV]
    cqn = ((cq * lax.rsqrt(jnp.mean(cq * cq, axis=0, keepdims=True) + EPS)) * qg_ref[...]).astype(BF16)
    for h in range(D_HEADS):
        qn = _dot(wqn_ref[h], cqn) * D_SCALE
        qp = _dot(wqp_ref[h], cqn)
        qpe = qp * td_ref[0] + _swap_row_halves(qp, D_ROPE) * td_ref[1]
        qd_ref[0, h, 0:64, :] = qn.astype(BF16)
        qd_ref[0, h, 64:96, :] = qpe.astype(BF16)
        qd_ref[0, h, 96:128, :] = jnp.zeros((32, tm), BF16)
    ckt = zq[T_DCKV:T_AV]
    cktn = ((ckt * lax.rsqrt(jnp.mean(ckt * ckt, axis=0, keepdims=True) + EPS)) * kvgc_ref[...]).astype(BF16)
    store_values(vd_ref, _dot(wvt_ref[...], cktn), D_HEADS)


def _rope_cs(s, dim):
    inv = 1.0 / (ROPE_THETA ** (jnp.arange(0, dim, 2, dtype=F32) / dim))
    ang = jnp.arange(s, dtype=F32)[:, None] * inv[None, :]
    cos, sin = jnp.cos(ang), jnp.sin(ang)
    return jnp.concatenate([cos, cos], -1), jnp.concatenate([-sin, sin], -1)


def _rope_tables(s):
    c32, s32 = _rope_cs(s, A_QK_DIM)
    c64, s64 = _rope_cs(s, B_HEAD_DIM)
    a_scale, b_scale, d_scale = A_QK_DIM ** -0.5 * LOG2E, B_HEAD_DIM ** -0.5 * LOG2E, D_SCALE
    ta = jnp.stack([c32.T, s32.T]) * a_scale
    tb = jnp.stack([c64.T, s64.T]) * b_scale
    td = jnp.stack([c32.T, s32.T]) * d_scale
    sa = jnp.stack([jnp.tile(c32, (1, 8)), jnp.tile(s32, (1, 8))])
    sb = jnp.stack([jnp.tile(c64, (1, 2)), jnp.tile(s64, (1, 2))])
    pad = jnp.zeros((s, 128 - D_ROPE), F32)
    sd = jnp.stack([jnp.concatenate([c32, pad], -1), jnp.concatenate([s32, pad], -1)])
    return ta, tb, td, sa, sb, sd


def _prep_layer(w_in, d_w_uq, d_w_ukv):
    pts = [0]
    for n in IN_SIZES:
        pts.append(pts[-1] + n)
    (aq, ak, av, bq, bk, bv, cq, ck, cv, dcq, dckv, dkr, gate) = [w_in[:, pts[i]:pts[i + 1]] for i in range(13)]
    wt = jnp.concatenate([aq, bq, cq, dcq, dckv, av, bv, cv, gate], axis=1).T.astype(BF16)
    zpad = jnp.zeros((D_MODEL, 128 - D_ROPE), F32)
    ws = jnp.concatenate([ak, bk, ck, dckv, dkr, zpad], axis=1).astype(BF16)
    uq = d_w_uq.reshape(D_Q_RANK, D_HEADS, D_NOPE + D_ROPE)
    uq_n, uq_p = uq[:, :, :D_NOPE], uq[:, :, D_NOPE:]
    wqn = jnp.transpose(uq_n, (1, 2, 0))
    wqp = jnp.transpose(uq_p, (1, 2, 0))
    ukv = d_w_ukv.reshape(D_KV_RANK, D_HEADS, D_NOPE + D_V)
    wkn = jnp.pad(ukv[:, :, :D_NOPE], ((0, 0), (0, 0), (0, 128 - D_NOPE))).reshape(D_KV_RANK, D_HEADS * 128)
    place = np.zeros((128, D_HEADS, 128), np.float32)
    place[np.arange(D_ROPE), :, D_NOPE + np.arange(D_ROPE)] = 1.0
    place = jnp.asarray(place.reshape(128, D_HEADS * 128), BF16)
    wvt = ukv[:, :, D_NOPE:].reshape(D_KV_RANK, D_HEADS * D_V).T
    return (wt, ws, wqn.astype(BF16), wqp.astype(BF16), wkn.astype(BF16), place, wvt.astype(BF16))


def _const_spec(shape):
    nd = len(shape)
    return pl.BlockSpec(shape, lambda *_: (0,) * nd)


def _proj(h, norm_g, prep, tabs, q_norm_g, kv_norm_g):
    b, s, _ = h.shape
    tm = min(TM, s)
    wt, ws, wqn, wqp, wkn, place, wvt = prep
    ta, tb, td, sa, sb, sd = tabs

    def tok(shape_tail):
        return pl.BlockSpec((1, 1, tm) + shape_tail, lambda i, j: (j, 0, i, 0))

    def chan(groups, rows):
        return pl.BlockSpec((1, groups, rows, tm), lambda i, j: (j, 0, 0, i))

    in_specs = [
        pl.BlockSpec((1, tm, D_MODEL), lambda i, j: (j, i, 0)),
        _const_spec((1, D_MODEL)), _const_spec(wt.shape), _const_spec(ws.shape),
        pl.BlockSpec((2, A_QK_DIM, tm), lambda i, j: (0, 0, i)),
        pl.BlockSpec((2, B_HEAD_DIM, tm), lambda i, j: (0, 0, i)),
        pl.BlockSpec((2, D_ROPE, tm), lambda i, j: (0, 0, i)),
        pl.BlockSpec((2, tm, 256), lambda i, j: (0, i, 0)),
        pl.BlockSpec((2, tm, 128), lambda i, j: (0, i, 0)),
        pl.BlockSpec((2, tm, 128), lambda i, j: (0, i, 0)),
        _const_spec((D_Q_RANK, 1)), _const_spec((D_KV_RANK, 1)), _const_spec((1, D_KV_RANK)),
        _const_spec(wqn.shape), _const_spec(wqp.shape),
        _const_spec(wkn.shape), _const_spec(place.shape), _const_spec(wvt.shape),
    ]
    out_shape = [
        jax.ShapeDtypeStruct((b, 8, 128, s), BF16),
        jax.ShapeDtypeStruct((b, 1, s, 256), BF16),
        jax.ShapeDtypeStruct((b, A_HEADS, V_ROWS, s), BF16),
        jax.ShapeDtypeStruct((b, B_HEADS, 128, s), BF16),
        jax.ShapeDtypeStruct((b, 1, s, 128), BF16),
        jax.ShapeDtypeStruct((b, B_KV_HEADS, V_ROWS, s), BF16),
        jax.ShapeDtypeStruct((b, C_HEADS, 128, s), BF16),
        jax.ShapeDtypeStruct((b, 1, s, 256), BF16),
        jax.ShapeDtypeStruct((b, C_HEADS, V_ROWS, s), BF16),
        jax.ShapeDtypeStruct((b, D_HEADS, 128, s), BF16),
        jax.ShapeDtypeStruct((b, 1, s, D_HEADS * 128), BF16),
        jax.ShapeDtypeStruct((b, D_HEADS, V_ROWS, s), BF16),
        jax.ShapeDtypeStruct((b, D_MIX, s), BF16),
    ]
    out_specs = [
        chan(8, 128), tok((256,)), chan(A_HEADS, V_ROWS),
        chan(B_HEADS, 128), tok((128,)), chan(B_KV_HEADS, V_ROWS),
        chan(C_HEADS, 128), tok((256,)), chan(C_HEADS, V_ROWS),
        chan(D_HEADS, 128), tok((D_HEADS * 128,)), chan(D_HEADS, V_ROWS),
        pl.BlockSpec((1, D_MIX, tm), lambda i, j: (j, 0, i)),
    ]
    return pl.pallas_call(
        _proj_kernel,
        grid=(s // tm, b),
        in_specs=in_specs, out_specs=out_specs, out_shape=out_shape,
        compiler_params=pltpu.CompilerParams(
            dimension_semantics=("arbitrary", "arbitrary"), vmem_limit_bytes=VMEM_LIMIT),
        name="proj",
    )(h, norm_g.reshape(1, D_MODEL), wt, ws, ta, tb, td, sa, sb, sd,
      q_norm_g.reshape(D_Q_RANK, 1), kv_norm_g.reshape(D_KV_RANK, 1), kv_norm_g.reshape(1, D_KV_RANK),
      wqn, wqp, wkn, place, wvt)


def _flash_kernel(k_ref, q_ref, v_ref, o_ref, s_buf, p_buf, pv_buf, *, tk, nk, tq):
    total = (q_ref.shape[3] // tq) * nk

    def scores(t):
        b, j = divmod(t, nk)
        s = _dot(k_ref[0, 0, j * tk:(j + 1) * tk, :], q_ref[0, 0, :, b * tq:(b + 1) * tq])
        s_buf[j] = s
        return jnp.max(s, axis=0, keepdims=True)

    def softmax(t, m, cmax):
        j = t % nk
        m_new = jnp.maximum(m, cmax)
        p_buf[j] = jnp.exp2(s_buf[j] - m_new).astype(BF16)
        return m_new, jnp.exp2(m - m_new)

    def values(t, acc, alpha):
        j = t % nk
        pv_buf[...] = _dot(v_ref[0, 0, :, j * tk:(j + 1) * tk], p_buf[j])
        return acc * alpha + pv_buf[...]

    m_init = jnp.full((1, tq), NEG_INF, F32)
    acc_init = jnp.zeros((V_ROWS, tq), F32)
    m, acc = m_init, acc_init
    cmax = {0: scores(0)}
    alpha = {}
    for t in range(total + 1):
        if t + 1 < total:
            cmax[t + 1] = scores(t + 1)
        if t < total:
            if t % nk == 0:
                m = m_init
            m, alpha[t] = softmax(t, m, cmax.pop(t))
        if t >= 1:
            if (t - 1) % nk == 0:
                acc = acc_init
            acc = values(t - 1, acc, alpha.pop(t - 1))
            if t % nk == 0:
                b = t // nk - 1
                o_ref[0, 0, :, b * tq:(b + 1) * tq] = (acc[0:64] / acc[64:65]).astype(o_ref.dtype)


def _flash(k, qt, vt, k_group, v_group, out_dtype):
    b, g, dk, s = qt.shape
    tq, tk = min(TQ_DENSE, s), min(TK, s)
    nk = s // tk
    slots = nk
    assert slots * tk * tq * 6 <= FLASH_BUF_BYTES
    tqs = tq * math.gcd(s // tq, FLASH_QBLOCKS)
    return pl.pallas_call(
        functools.partial(_flash_kernel, tk=tk, nk=nk, tq=tq),
        grid=(b, g, s // tqs),
        scratch_shapes=[pltpu.VMEM((slots, tk, tq), F32), pltpu.VMEM((slots, tk, tq), BF16),
                        pltpu.VMEM((V_ROWS, tq), F32)],
        in_specs=[
            pl.BlockSpec((1, 1, s, dk), lambda bi, gi, i: (bi,) + k_group(gi)),
            pl.BlockSpec((1, 1, dk, tqs), lambda bi, gi, i: (bi, gi, 0, i)),
            pl.BlockSpec((1, 1, V_ROWS, s), lambda bi, gi, i: (bi, v_group(gi), 0, 0)),
        ],
        out_specs=pl.BlockSpec((1, 1, 64, tqs), lambda bi, gi, i: (bi, gi, 0, i)),
        out_shape=jax.ShapeDtypeStruct((b, g, 64, s), out_dtype),
        compiler_params=pltpu.CompilerParams(
            dimension_semantics=("arbitrary", "arbitrary", "arbitrary"), vmem_limit_bytes=VMEM_LIMIT),
        name="flash",
    )(k, qt, vt)


def _banded_kernel(sink_ref, bias_ref, kb_ref, qb_ref, vb_ref, kc_ref, qc_ref, vc_ref, ob_ref, oc_ref, s_buf,
                   *, s_len, win_b, win_c):
    i = pl.program_id(1)
    tq = qb_ref.shape[3]
    q0 = i * tq
    start_b = pl.multiple_of(jnp.clip(q0 - B_WINDOW, 0, s_len - win_b), 128)
    start_c = pl.multiple_of(jnp.clip(q0 - (C_MAX_ROWS // 2) * GRID_W, 0, s_len - win_c), 256)
    kb = kb_ref[0, 0, pl.ds(start_b, win_b), :]
    kpos = start_b + lax.broadcasted_iota(jnp.int32, (win_b, tq), 0)
    qpos = q0 + lax.broadcasted_iota(jnp.int32, (win_b, tq), 1)
    in_band = jnp.abs(kpos - qpos) <= B_WINDOW
    sc_b = [_dot(kb, qb_ref[0, h]) for h in range(B_HEADS)]
    for h in range(C_HEADS):
        k = kc_ref[0, 0, pl.ds(start_c, win_c), 128 * (h // 2):128 * (h // 2) + 128]
        s_buf[h] = _dot(k, qc_ref[0, h])
    for h in range(B_HEADS):
        s = jnp.where(in_band, sc_b[h], NEG_INF)
        sink = sink_ref[h] * LOG2E
        m = jnp.maximum(jnp.max(s, axis=0, keepdims=True), sink)
        p = jnp.exp2(s - m).astype(BF16)
        acc = _dot(vb_ref[0, h // (B_HEADS // B_KV_HEADS), :, pl.ds(start_b, win_b)], p)
        ob_ref[0, h] = (acc[0:64] / (acc[64:65] + jnp.exp2(sink - m))).astype(ob_ref.dtype)
        s = s_buf[h] + bias_ref[0, h]
        m = jnp.max(s, axis=0, keepdims=True)
        p = jnp.exp2(s - m).astype(BF16)
        acc = _dot(vc_ref[0, h, :, pl.ds(start_c, win_c)], p)
        oc_ref[0, h] = (acc[0:64] / acc[64:65]).astype(oc_ref.dtype)


def _nbr_bias(rpb, rows):
    col = np.arange(GRID_W)
    cs = np.clip(col - C_WIN_COLS // 2, 0, GRID_W - C_WIN_COLS)
    col_ok = (col[:, None] >= cs[None, :]) & (col[:, None] < cs[None, :] + C_WIN_COLS)
    dc = np.clip(col[:, None] - col[None, :], -(C_WIN_COLS - 1), C_WIN_COLS - 1) + (C_WIN_COLS - 1)
    rp = rpb.astype(F32)
    by_cols = jnp.zeros(rp.shape[:2] + dc.shape, F32)
    for b in range(rp.shape[2]):
        by_cols = by_cols + jnp.where((dc == b)[None, None], rp[:, :, b][:, :, None, None], 0.0)
    kr = min(C_MAX_ROWS, rows)
    out = []
    for r0 in (0, C_QROWS, rows - C_QROWS):
        rk0 = min(max(r0 - C_MAX_ROWS // 2, 0), rows - C_KROWS)
        rq = r0 + np.arange(C_QROWS)
        rk = rk0 + np.arange(C_KROWS)
        rs = np.clip(rq - kr // 2, 0, rows - kr)
        row_ok = (rk[:, None] >= rs[None, :]) & (rk[:, None] < rs[None, :] + kr)
        dr = np.clip(rk[:, None] - rq[None, :] + (C_MAX_ROWS - 1), 0, 2 * C_MAX_ROWS - 2)
        blocks = jnp.stack([jnp.stack([by_cols[:, dr[a, c]] for c in range(C_QROWS)], axis=2)
                            for a in range(C_KROWS)], axis=1)
        ok = row_ok[:, None, :, None] & col_ok[None, :, None, :]
        bias = jnp.where(ok[None], blocks * LOG2E, NEG_INF)
        out.append(bias.reshape(rp.shape[0], C_KROWS * GRID_W, C_QROWS * GRID_W))
    return jnp.stack(out)


def _banded(kb, qb, vb, sink, kc, qc, vc, rpb):
    b, g, dk, s = qb.shape
    tq = C_QROWS * GRID_W
    win_b = tq + 2 * B_WINDOW
    win_c = C_KROWS * GRID_W
    nq = s // tq
    bias = _nbr_bias(rpb, s // GRID_W)

    def variant(i):
        return jnp.where(i == 0, 0, jnp.where(i == nq - 1, 2, 1))

    qspec = pl.BlockSpec((1, g, dk, tq), lambda bi, i: (bi, 0, 0, i))
    ospec = pl.BlockSpec((1, g, 64, tq), lambda bi, i: (bi, 0, 0, i))
    return pl.pallas_call(
        functools.partial(_banded_kernel, s_len=s, win_b=win_b, win_c=win_c),
        grid=(b, nq),
        scratch_shapes=[pltpu.VMEM((g, win_c, tq), F32)],
        in_specs=[
            pl.BlockSpec(memory_space=pltpu.SMEM),
            pl.BlockSpec((1, g, win_c, tq), lambda bi, i: (variant(i), 0, 0, 0)),
            pl.BlockSpec((1, 1, s, kb.shape[3]), lambda bi, i: (bi, 0, 0, 0)), qspec,
            pl.BlockSpec((1, B_KV_HEADS, V_ROWS, s), lambda bi, i: (bi, 0, 0, 0)),
            pl.BlockSpec((1, 1, s, kc.shape[3]), lambda bi, i: (bi, 0, 0, 0)), qspec,
            pl.BlockSpec((1, g, V_ROWS, s), lambda bi, i: (bi, 0, 0, 0)),
        ],
        out_specs=[ospec, ospec],
        out_shape=[jax.ShapeDtypeStruct((b, g, 64, s), BF16)] * 2,
        compiler_params=pltpu.CompilerParams(
            dimension_semantics=("arbitrary", "arbitrary"), vmem_limit_bytes=VMEM_LIMIT),
        name="banded",
    )(sink.astype(F32), bias, kb, qb, vb, kc, qc, vc)


def _out_kernel(oa_ref, ob_ref, oc_ref, od_ref, sg_ref, h_ref, p_ref, lam_ref, sub_ref,
                wo_ref, pg_ref, wg_ref, wp_ref, fg_ref, y_ref, *, lam_init, final):
    ple = _dot(p_ref[0, 0].astype(BF16), wp_ref[...])
    lv = lam_ref[...]
    lam = (jnp.exp(jnp.sum(lv[0:1] * lv[1:2], axis=-1, keepdims=True))
           - jnp.exp(jnp.sum(lv[2:3] * lv[3:4], axis=-1, keepdims=True)) + lam_init)
    parts = []
    for h in range(A_HEADS):
        o = oa_ref[0, 2 * h] - lam * oa_ref[0, 2 * h + 1]
        on = (o * lax.rsqrt(jnp.mean(o * o, axis=0, keepdims=True) + EPS)) * sub_ref[...]
        parts.append(on * (1.0 - lam_init))
    for ref in (ob_ref, oc_ref, od_ref):
        for h in range(4):
            parts.append(ref[0, h].astype(F32))
    mix = (jnp.concatenate(parts, axis=0) * sg_ref[0]).astype(BF16)
    h1 = h_ref[0] + _tn(mix, wo_ref[...])
    hn = ((h1 * lax.rsqrt(jnp.mean(h1 * h1, axis=-1, keepdims=True) + EPS)) * pg_ref[...]).astype(BF16)
    gate = jax.nn.sigmoid(_dot(hn, wg_ref[...]))
    h2 = h1 + gate * ple
    if final:
        h2 = (h2 * lax.rsqrt(jnp.mean(h2 * h2, axis=-1, keepdims=True) + EPS)) * fg_ref[...]
    y_ref[0] = h2


def _out(oa, ob, oc, od, sg, h, p, layer, a_lambda, subln_g, w_out, ple_norm_g, w_ple_gate, w_ple_proj,
         final_norm_g, lam_init, final):
    b, s, _ = h.shape
    tm = min(TM, s)

    def heads(n):
        return pl.BlockSpec((1, n, 64, tm), lambda bi, i: (bi, 0, 0, i))

    return pl.pallas_call(
        functools.partial(_out_kernel, lam_init=lam_init, final=final),
        grid=(b, s // tm),
        in_specs=[
            heads(8), heads(4), heads(4), heads(4),
            pl.BlockSpec((1, D_MIX, tm), lambda bi, i: (bi, 0, i)),
            pl.BlockSpec((1, tm, D_MODEL), lambda bi, i: (bi, i, 0)),
            pl.BlockSpec((1, 1, tm, D_PLE), lambda bi, i: (layer, bi, i, 0)),
            _const_spec((4, A_QK_DIM)), _const_spec((A_V_DIM, 1)),
            _const_spec((D_MIX, D_MODEL)), _const_spec((1, D_MODEL)),
            _const_spec((D_MODEL, D_MODEL)), _const_spec((D_PLE, D_MODEL)), _const_spec((1, D_MODEL)),
        ],
        out_specs=pl.BlockSpec((1, tm, D_MODEL), lambda bi, i: (bi, i, 0)),
        out_shape=jax.ShapeDtypeStruct((b, s, D_MODEL), F32),
        compiler_params=pltpu.CompilerParams(
            dimension_semantics=("arbitrary", "arbitrary"), vmem_limit_bytes=VMEM_LIMIT),
        name="out",
    )(oa, ob, oc, od, sg, h, p, a_lambda.astype(F32), subln_g.reshape(A_V_DIM, 1).astype(F32),
      w_out.astype(BF16), ple_norm_g.reshape(1, D_MODEL), w_ple_gate.astype(BF16), w_ple_proj.astype(BF16),
      final_norm_g.reshape(1, D_MODEL))


def _trunk(x, p, tabs, preps, norm_g, a_lambda, a_subln_g, b_sink, c_rpb, d_q_norm_g, d_kv_norm_g,
           w_out, ple_norm_g, w_ple_gate, w_ple_proj, final_norm_g):
    depth = len(preps)
    s = x.shape[1]
    ta, tb, td, sa, sb, sd = tabs
    tabs = (ta[:, :, :s], tb[:, :, :s], td[:, :, :s], sa[:, :s], sb[:, :s], sd[:, :s])
    h = x
    for i in range(depth):
        (qa, ka, va, qb, kb, vb, qc, kc, vc, qd, kd, vd, sg) = _proj(
            h, norm_g[i], preps[i], tabs, d_q_norm_g[i], d_kv_norm_g[i])
        oa = _flash(ka, qa, va, lambda g: (0, 0, g // 4), lambda g: g // 2, F32)
        od = _flash(kd, qd, vd, lambda g: (0, 0, g), lambda g: g, BF16)
        ob, oc = _banded(kb, qb, vb, b_sink[i], kc, qc, vc, c_rpb[i])
        lam_init = 0.8 - 0.6 * math.exp(-0.3 * i)
        h = _out(oa, ob, oc, od, sg, h, p, i, a_lambda[i], a_subln_g[i], w_out[i], ple_norm_g[i],
                 w_ple_gate[i], w_ple_proj[i], final_norm_g, lam_init, i == depth - 1)
    return h


def kernel(x_prompt, x_sample, p_prompt, p_sample, norm_g, w_in, a_lambda, a_subln_g, b_sink, c_rpb,
           d_q_norm_g, d_kv_norm_g, d_w_uq, d_w_ukv, w_out, ple_norm_g, w_ple_gate, w_ple_proj,
           final_norm_g):
    tabs = _rope_tables(max(x_prompt.shape[1], x_sample.shape[1]))
    preps = [_prep_layer(w_in[i], d_w_uq[i], d_w_ukv[i]) for i in range(w_in.shape[0])]
    weights = (norm_g, a_lambda, a_subln_g, b_sink, c_rpb, d_q_norm_g, d_kv_norm_g,
               w_out, ple_norm_g, w_ple_gate, w_ple_proj, final_norm_g)
    return (_trunk(x_prompt, p_prompt, tabs, preps, *weights), _trunk(x_sample, p_sample, tabs, preps, *weights))
```

```python
import functools
import math

import jax
import jax.numpy as jnp
import numpy as np
from jax import lax
from jax.experimental import pallas as pl
from jax.experimental.pallas import tpu as pltpu

F32 = jnp.float32
BF16 = jnp.bfloat16

D_MODEL = 1024
D_MIX = 1024
D_PLE = 256
GRID_W = 64
ROPE_THETA = 10000.0
EPS = 1e-6
NEG_INF = -1e30

A_HEADS, A_QK_DIM, A_V_DIM = 4, 32, 64
B_HEADS, B_KV_HEADS, B_HEAD_DIM, B_WINDOW = 4, 2, 64, 128
C_HEADS, C_HEAD_DIM, C_MAX_ROWS, C_WIN_COLS = 4, 64, 8, 16
D_HEADS, D_Q_RANK, D_KV_RANK, D_NOPE, D_ROPE, D_V = 4, 256, 128, 64, 32, 64

IN_SIZES = (256, 256, 256, 256, 128, 128, 256, 256, 256, D_Q_RANK, D_KV_RANK, D_ROPE, D_MIX)

V_ROWS = 80
VMEM_LIMIT = 56 * 1024 * 1024

TM = 512
H_SLOTS = 3
TQ_DENSE = 512
LOG2E = math.log2(math.e)
D_SCALE = (D_NOPE + D_ROPE) ** -0.5 * LOG2E
TK = 256
FLASH_QBLOCKS = 8
FLASH_BUF_BYTES = 24 * 1024 * 1024
C_QROWS = 4
C_KROWS = 12


def _nt(a, b):
    return lax.dot_general(a, b, (((1,), (1,)), ((), ())), preferred_element_type=F32)


def _tn(a, b):
    return lax.dot_general(a, b, (((0,), (0,)), ((), ())), preferred_element_type=F32)


def _dot(a, b):
    return jnp.dot(a, b, preferred_element_type=F32)


T_AQ, T_BQ, T_CQ, T_DCQ, T_DCKV, T_AV, T_BV, T_CV, T_GATE, T_END = (
    0, 256, 512, 768, 1024, 1152, 1408, 1536, 1792, 2816)
S_AK, S_BK, S_CK, S_DCKV, S_KPE, S_END = (0, 256, 384, 640, 768, 896)


def _swap_row_halves(x, d):
    h = d // 2
    return jnp.concatenate([x[r + o:r + o + h] for r in range(0, x.shape[0], d) for o in (h, 0)], axis=0)


def _swap_lane_halves(x, d):
    n = x.shape[1]
    lane = lax.broadcasted_iota(jnp.int32, x.shape, 1)
    return jnp.where(lane % d < d // 2, pltpu.roll(x, n - d // 2, 1), pltpu.roll(x, d // 2, 1))


def _proj_kernel(h_ref, g_ref, wt_ref, ws_ref, ta_ref, tb_ref, td_ref, sa_ref, sb_ref, sd_ref,
                 qg_ref, kvgc_ref, kvgr_ref, wqn_ref, wqp_ref, wkn_ref, place_ref, wvt_ref,
                 qa_ref, ka_ref, va_ref, qb_ref, kb_ref, vb_ref, qc_ref, kc_ref, vc_ref,
                 qd_ref, kd_ref, vd_ref, sg_ref):
    x = h_ref[0]
    tm = x.shape[0]
    ms = jnp.mean(x * x, axis=-1, keepdims=True)
    hn = ((x * lax.rsqrt(ms + EPS)) * g_ref[...]).astype(BF16)

    zs = _dot(hn, ws_ref[...])
    zak, zbk, zpe = zs[:, S_AK:S_BK], zs[:, S_BK:S_CK], zs[:, S_KPE:S_END]
    ka_ref[0, 0] = (zak * sa_ref[0] + _swap_lane_halves(zak, A_QK_DIM) * sa_ref[1]).astype(BF16)
    kb_ref[0, 0] = (zbk * sb_ref[0] + _swap_lane_halves(zbk, B_HEAD_DIM) * sb_ref[1]).astype(BF16)
    kc_ref[0, 0] = zs[:, S_CK:S_DCKV].astype(BF16)
    ck = zs[:, S_DCKV:S_KPE]
    ckn = ((ck * lax.rsqrt(jnp.mean(ck * ck, axis=-1, keepdims=True) + EPS)) * kvgr_ref[...]).astype(BF16)
    kpe = (zpe * sd_ref[0] + _swap_lane_halves(zpe, D_ROPE) * sd_ref[1]).astype(BF16)
    kd_ref[0, 0] = (_dot(ckn, wkn_ref[...]) + _dot(kpe, place_ref[...])).astype(BF16)

    zq = _nt(wt_ref[T_AQ:T_AV, :], hn)
    zv = _nt(wt_ref[T_AV:T_GATE, :], hn)
    gt = _nt(wt_ref[T_GATE:T_END, :], hn)
    sg_ref[0] = (gt * jax.nn.sigmoid(gt)).astype(BF16)

    ones_rows = (lax.broadcasted_iota(jnp.int32, (V_ROWS - 64, tm), 0) == 0).astype(BF16)

    def store_values(ref, v, heads):
        for h in range(heads):
            ref[0, h, 0:64, :] = v[64 * h:64 * h + 64].astype(BF16)
            ref[0, h, 64:V_ROWS, :] = ones_rows

    za = zq[T_AQ:T_BQ]
    qa = (za * jnp.tile(ta_ref[0], (2 * A_HEADS, 1))
          + _swap_row_halves(za, A_QK_DIM) * jnp.tile(ta_ref[1], (2 * A_HEADS, 1)))
    for g in range(2 * A_HEADS):
        r = 32 * (g % 4)
        qa_ref[0, g] = jnp.zeros((128, tm), BF16)
        qa_ref[0, g, r:r + 32, :] = qa[32 * g:32 * g + 32].astype(BF16)
    store_values(va_ref, zv[0:T_BV - T_AV], A_HEADS)

    zb = zq[T_BQ:T_CQ]
    qb = (zb * jnp.tile(tb_ref[0], (B_HEADS, 1))
          + _swap_row_halves(zb, B_HEAD_DIM) * jnp.tile(tb_ref[1], (B_HEADS, 1)))
    for h in range(B_HEADS):
        r = 64 * (h // 2)
        qb_ref[0, h] = jnp.zeros((128, tm), BF16)
        qb_ref[0, h, r:r + 64, :] = qb[64 * h:64 * h + 64].astype(BF16)
    store_values(vb_ref, zv[T_BV - T_AV:T_CV - T_AV], B_KV_HEADS)

    qc = zq[T_CQ:T_DCQ] * (C_HEAD_DIM ** -0.5 * LOG2E)
    for h in range(C_HEADS):
        r = 64 * (h % 2)
        qc_ref[0, h] = jnp.zeros((128, tm), BF16)
        qc_ref[0, h, r:r + 64, :] = qc[64 * h:64 * h + 64].astype(BF16)
    store_values(vc_ref, zv[T_CV - T_AV:T_GATE - T_AV], C_HEADS)

    cq = zq[T_DCQ:T_DCKV]
    cqn = ((cq * lax.rsqrt(jnp.mean(cq * cq, axis=0, keepdims=True) + EPS)) * qg_ref[...]).astype(BF16)
    for h in range(D_HEADS):
        qn = _dot(wqn_ref[h], cqn) * D_SCALE
        qp = _dot(wqp_ref[h], cqn)
        qpe = qp * td_ref[0] + _swap_row_halves(qp, D_ROPE) * td_ref[1]
        qd_ref[0, h, 0:64, :] = qn.astype(BF16)
        qd_ref[0, h, 64:96, :] = qpe.astype(BF16)
        qd_ref[0, h, 96:128, :] = jnp.zeros((32, tm), BF16)
    ckt = zq[T_DCKV:T_AV]
    cktn = ((ckt * lax.rsqrt(jnp.mean(ckt * ckt, axis=0, keepdims=True) + EPS)) * kvgc_ref[...]).astype(BF16)
    store_values(vd_ref, _dot(wvt_ref[...], cktn), D_HEADS)


def _rope_cs(s, dim):
    inv = 1.0 / (ROPE_THETA ** (jnp.arange(0, dim, 2, dtype=F32) / dim))
    ang = jnp.arange(s, dtype=F32)[:, None] * inv[None, :]
    cos, sin = jnp.cos(ang), jnp.sin(ang)
    return jnp.concatenate([cos, cos], -1), jnp.concatenate([-sin, sin], -1)


def _rope_tables(s):
    c32, s32 = _rope_cs(s, A_QK_DIM)
    c64, s64 = _rope_cs(s, B_HEAD_DIM)
    a_scale, b_scale, d_scale = A_QK_DIM ** -0.5 * LOG2E, B_HEAD_DIM ** -0.5 * LOG2E, D_SCALE
    ta = jnp.stack([c32.T, s32.T]) * a_scale
    tb = jnp.stack([c64.T, s64.T]) * b_scale
    td = jnp.stack([c32.T, s32.T]) * d_scale
    sa = jnp.stack([jnp.tile(c32, (1, 8)), jnp.tile(s32, (1, 8))])
    sb = jnp.stack([jnp.tile(c64, (1, 2)), jnp.tile(s64, (1, 2))])
    pad = jnp.zeros((s, 128 - D_ROPE), F32)
    sd = jnp.stack([jnp.concatenate([c32, pad], -1), jnp.concatenate([s32, pad], -1)])
    return ta, tb, td, sa, sb, sd


def _prep_layer(w_in, d_w_uq, d_w_ukv):
    pts = [0]
    for n in IN_SIZES:
        pts.append(pts[-1] + n)
    (aq, ak, av, bq, bk, bv, cq, ck, cv, dcq, dckv, dkr, gate) = [w_in[:, pts[i]:pts[i + 1]] for i in range(13)]
    wt = jnp.concatenate([aq, bq, cq, dcq, dckv, av, bv, cv, gate], axis=1).T.astype(BF16)
    zpad = jnp.zeros((D_MODEL, 128 - D_ROPE), F32)
    ws = jnp.concatenate([ak, bk, ck, dckv, dkr, zpad], axis=1).astype(BF16)
    uq = d_w_uq.reshape(D_Q_RANK, D_HEADS, D_NOPE + D_ROPE)
    uq_n, uq_p = uq[:, :, :D_NOPE], uq[:, :, D_NOPE:]
    wqn = jnp.transpose(uq_n, (1, 2, 0))
    wqp = jnp.transpose(uq_p, (1, 2, 0))
    ukv = d_w_ukv.reshape(D_KV_RANK, D_HEADS, D_NOPE + D_V)
    wkn = jnp.pad(ukv[:, :, :D_NOPE], ((0, 0), (0, 0), (0, 128 - D_NOPE))).reshape(D_KV_RANK, D_HEADS * 128)
    place = np.zeros((128, D_HEADS, 128), np.float32)
    place[np.arange(D_ROPE), :, D_NOPE + np.arange(D_ROPE)] = 1.0
    place = jnp.asarray(place.reshape(128, D_HEADS * 128), BF16)
    wvt = ukv[:, :, D_NOPE:].reshape(D_KV_RANK, D_HEADS * D_V).T
    return (wt, ws, wqn.astype(BF16), wqp.astype(BF16), wkn.astype(BF16), place, wvt.astype(BF16))


def _const_spec(shape):
    nd = len(shape)
    return pl.BlockSpec(shape, lambda *_: (0,) * nd)


def _proj(h, norm_g, prep, tabs, q_norm_g, kv_norm_g):
    b, s, _ = h.shape
    tm = min(TM, s)
    wt, ws, wqn, wqp, wkn, place, wvt = prep
    ta, tb, td, sa, sb, sd = tabs

    def tok(shape_tail):
        return pl.BlockSpec((1, 1, tm) + shape_tail, lambda i, j: (j, 0, i, 0))

    def chan(groups, rows):
        return pl.BlockSpec((1, groups, rows, tm), lambda i, j: (j, 0, 0, i))

    in_specs = [
        pl.BlockSpec((1, tm, D_MODEL), lambda i, j: (j, i, 0)),
        _const_spec((1, D_MODEL)), _const_spec(wt.shape), _const_spec(ws.shape),
        pl.BlockSpec((2, A_QK_DIM, tm), lambda i, j: (0, 0, i)),
        pl.BlockSpec((2, B_HEAD_DIM, tm), lambda i, j: (0, 0, i)),
        pl.BlockSpec((2, D_ROPE, tm), lambda i, j: (0, 0, i)),
        pl.BlockSpec((2, tm, 256), lambda i, j: (0, i, 0)),
        pl.BlockSpec((2, tm, 128), lambda i, j: (0, i, 0)),
        pl.BlockSpec((2, tm, 128), lambda i, j: (0, i, 0)),
        _const_spec((D_Q_RANK, 1)), _const_spec((D_KV_RANK, 1)), _const_spec((1, D_KV_RANK)),
        _const_spec(wqn.shape), _const_spec(wqp.shape),
        _const_spec(wkn.shape), _const_spec(place.shape), _const_spec(wvt.shape),
    ]
    out_shape = [
        jax.ShapeDtypeStruct((b, 8, 128, s), BF16),
        jax.ShapeDtypeStruct((b, 1, s, 256), BF16),
        jax.ShapeDtypeStruct((b, A_HEADS, V_ROWS, s), BF16),
        jax.ShapeDtypeStruct((b, B_HEADS, 128, s), BF16),
        jax.ShapeDtypeStruct((b, 1, s, 128), BF16),
        jax.ShapeDtypeStruct((b, B_KV_HEADS, V_ROWS, s), BF16),
        jax.ShapeDtypeStruct((b, C_HEADS, 128, s), BF16),
        jax.ShapeDtypeStruct((b, 1, s, 256), BF16),
        jax.ShapeDtypeStruct((b, C_HEADS, V_ROWS, s), BF16),
        jax.ShapeDtypeStruct((b, D_HEADS, 128, s), BF16),
        jax.ShapeDtypeStruct((b, 1, s, D_HEADS * 128), BF16),
        jax.ShapeDtypeStruct((b, D_HEADS, V_ROWS, s), BF16),
        jax.ShapeDtypeStruct((b, D_MIX, s), BF16),
    ]
    out_specs = [
        chan(8, 128), tok((256,)), chan(A_HEADS, V_ROWS),
        chan(B_HEADS, 128), tok((128,)), chan(B_KV_HEADS, V_ROWS),
        chan(C_HEADS, 128), tok((256,)), chan(C_HEADS, V_ROWS),
        chan(D_HEADS, 128), tok((D_HEADS * 128,)), chan(D_HEADS, V_ROWS),
        pl.BlockSpec((1, D_MIX, tm), lambda i, j: (j, 0, i)),
    ]
    return pl.pallas_call(
        _proj_kernel,
        grid=(s // tm, b),
        in_specs=in_specs, out_specs=out_specs, out_shape=out_shape,
        compiler_params=pltpu.CompilerParams(
            dimension_semantics=("arbitrary", "arbitrary"), vmem_limit_bytes=VMEM_LIMIT),
        name="proj",
    )(h, norm_g.reshape(1, D_MODEL), wt, ws, ta, tb, td, sa, sb, sd,
      q_norm_g.reshape(D_Q_RANK, 1), kv_norm_g.reshape(D_KV_RANK, 1), kv_norm_g.reshape(1, D_KV_RANK),
      wqn, wqp, wkn, place, wvt)


def _flash_kernel(k_ref, q_ref, v_ref, o_ref, s_buf, p_buf, pv_buf, *, tk, nk, tq):
    total = (q_ref.shape[3] // tq) * nk

    def scores(t):
        b, j = divmod(t, nk)
        s = _dot(k_ref[0, 0, j * tk:(j + 1) * tk, :], q_ref[0, 0, :, b * tq:(b + 1) * tq])
        s_buf[j] = s
        return jnp.max(s, axis=0, keepdims=True)

    def softmax(t, m, cmax):
        j = t % nk
        m_new = jnp.maximum(m, cmax)
        p_buf[j] = jnp.exp2(s_buf[j] - m_new).astype(BF16)
        return m_new, jnp.exp2(m - m_new)

    def values(t, acc, alpha):
        j = t % nk
        pv_buf[...] = _dot(v_ref[0, 0, :, j * tk:(j + 1) * tk], p_buf[j])
        return acc * alpha + pv_buf[...]

    m_init = jnp.full((1, tq), NEG_INF, F32)
    acc_init = jnp.zeros((V_ROWS, tq), F32)
    m, acc = m_init, acc_init
    cmax = {0: scores(0)}
    alpha = {}
    for t in range(total + 1):
        if t + 1 < total:
            cmax[t + 1] = scores(t + 1)
        if t < total:
            if t % nk == 0:
                m = m_init
            m, alpha[t] = softmax(t, m, cmax.pop(t))
        if t >= 1:
            if (t - 1) % nk == 0:
                acc = acc_init
            acc = values(t - 1, acc, alpha.pop(t - 1))
            if t % nk == 0:
                b = t // nk - 1
                o_ref[0, 0, :, b * tq:(b + 1) * tq] = (acc[0:64] / acc[64:65]).astype(o_ref.dtype)


def _flash(k, qt, vt, k_group, v_group, out_dtype):
    b, g, dk, s = qt.shape
    tq, tk = min(TQ_DENSE, s), min(TK, s)
    nk = s // tk
    slots = nk
    assert slots * tk * tq * 6 <= FLASH_BUF_BYTES
    tqs = tq * math.gcd(s // tq, FLASH_QBLOCKS)
    return pl.pallas_call(
        functools.partial(_flash_kernel, tk=tk, nk=nk, tq=tq),
        grid=(b, g, s // tqs),
        scratch_shapes=[pltpu.VMEM((slots, tk, tq), F32), pltpu.VMEM((slots, tk, tq), BF16),
                        pltpu.VMEM((V_ROWS, tq), F32)],
        in_specs=[
            pl.BlockSpec((1, 1, s, dk), lambda bi, gi, i: (bi,) + k_group(gi)),
            pl.BlockSpec((1, 1, dk, tqs), lambda bi, gi, i: (bi, gi, 0, i)),
            pl.BlockSpec((1, 1, V_ROWS, s), lambda bi, gi, i: (bi, v_group(gi), 0, 0)),
        ],
        out_specs=pl.BlockSpec((1, 1, 64, tqs), lambda bi, gi, i: (bi, gi, 0, i)),
        out_shape=jax.ShapeDtypeStruct((b, g, 64, s), out_dtype),
        compiler_params=pltpu.CompilerParams(
            dimension_semantics=("arbitrary", "arbitrary", "arbitrary"), vmem_limit_bytes=VMEM_LIMIT),
        name="flash",
    )(k, qt, vt)


def _banded_kernel(sink_ref, bias_ref, kb_ref, qb_ref, vb_ref, kc_ref, qc_ref, vc_ref, ob_ref, oc_ref, s_buf,
                   *, s_len, win_b, win_c):
    i = pl.program_id(1)
    tq = qb_ref.shape[3]
    q0 = i * tq
    start_b = pl.multiple_of(jnp.clip(q0 - B_WINDOW, 0, s_len - win_b), 128)
    start_c = pl.multiple_of(jnp.clip(q0 - (C_MAX_ROWS // 2) * GRID_W, 0, s_len - win_c), 256)
    kb = kb_ref[0, 0, pl.ds(start_b, win_b), :]
    kpos = start_b + lax.broadcasted_iota(jnp.int32, (win_b, tq), 0)
    qpos = q0 + lax.broadcasted_iota(jnp.int32, (win_b, tq), 1)
    in_band = jnp.abs(kpos - qpos) <= B_WINDOW
    sc_b = [_dot(kb, qb_ref[0, h]) for h in range(B_HEADS)]
    for h in range(C_HEADS):
        k = kc_ref[0, 0, pl.ds(start_c, win_c), 128 * (h // 2):128 * (h // 2) + 128]
        s_buf[h] = _dot(k, qc_ref[0, h])
    for h in range(B_HEADS):
        s = jnp.where(in_band, sc_b[h], NEG_INF)
        sink = sink_ref[h] * LOG2E
        m = jnp.maximum(jnp.max(s, axis=0, keepdims=True), sink)
        p = jnp.exp2(s - m).astype(BF16)
        acc = _dot(vb_ref[0, h // (B_HEADS // B_KV_HEADS), :, pl.ds(start_b, win_b)], p)
        ob_ref[0, h] = (acc[0:64] / (acc[64:65] + jnp.exp2(sink - m))).astype(ob_ref.dtype)
        s = s_buf[h] + bias_ref[0, h]
        m = jnp.max(s, axis=0, keepdims=True)
        p = jnp.exp2(s - m).astype(BF16)
        acc = _dot(vc_ref[0, h, :, pl.ds(start_c, win_c)], p)
        oc_ref[0, h] = (acc[0:64] / acc[64:65]).astype(oc_ref.dtype)


def _nbr_bias(rpb, rows):
    col = np.arange(GRID_W)
    cs = np.clip(col - C_WIN_COLS // 2, 0, GRID_W - C_WIN_COLS)
    col_ok = (col[:, None] >= cs[None, :]) & (col[:, None] < cs[None, :] + C_WIN_COLS)
    dc = np.clip(col[:, None] - col[None, :], -(C_WIN_COLS - 1), C_WIN_COLS - 1) + (C_WIN_COLS - 1)
    rp = rpb.astype(F32)
    by_cols = jnp.zeros(rp.shape[:2] + dc.shape, F32)
    for b in range(rp.shape[2]):
        by_cols = by_cols + jnp.where((dc == b)[None, None], rp[:, :, b][:, :, None, None], 0.0)
    kr = min(C_MAX_ROWS, rows)
    out = []
    for r0 in (0, C_QROWS, rows - C_QROWS):
        rk0 = min(max(r0 - C_MAX_ROWS // 2, 0), rows - C_KROWS)
        rq = r0 + np.arange(C_QROWS)
        rk = rk0 + np.arange(C_KROWS)
        rs = np.clip(rq - kr // 2, 0, rows - kr)
        row_ok = (rk[:, None] >= rs[None, :]) & (rk[:, None] < rs[None, :] + kr)
        dr = np.clip(rk[:, None] - rq[None, :] + (C_MAX_ROWS - 1), 0, 2 * C_MAX_ROWS - 2)
        blocks = jnp.stack([jnp.stack([by_cols[:, dr[a, c]] for c in range(C_QROWS)], axis=2)
                            for a in range(C_KROWS)], axis=1)
        ok = row_ok[:, None, :, None] & col_ok[None, :, None, :]
        bias = jnp.where(ok[None], blocks * LOG2E, NEG_INF)
        out.append(bias.reshape(rp.shape[0], C_KROWS * GRID_W, C_QROWS * GRID_W))
    return jnp.stack(out)


def _banded(kb, qb, vb, sink, kc, qc, vc, rpb):
    b, g, dk, s = qb.shape
    tq = C_QROWS * GRID_W
    win_b = tq + 2 * B_WINDOW
    win_c = C_KROWS * GRID_W
    nq = s // tq
    bias = _nbr_bias(rpb, s // GRID_W)

    def variant(i):
        return jnp.where(i == 0, 0, jnp.where(i == nq - 1, 2, 1))

    qspec = pl.BlockSpec((1, g, dk, tq), lambda bi, i: (bi, 0, 0, i))
    ospec = pl.BlockSpec((1, g, 64, tq), lambda bi, i: (bi, 0, 0, i))
    return pl.pallas_call(
        functools.partial(_banded_kernel, s_len=s, win_b=win_b, win_c=win_c),
        grid=(b, nq),
        scratch_shapes=[pltpu.VMEM((g, win_c, tq), F32)],
        in_specs=[
            pl.BlockSpec(memory_space=pltpu.SMEM),
            pl.BlockSpec((1, g, win_c, tq), lambda bi, i: (variant(i), 0, 0, 0)),
            pl.BlockSpec((1, 1, s, kb.shape[3]), lambda bi, i: (bi, 0, 0, 0)), qspec,
            pl.BlockSpec((1, B_KV_HEADS, V_ROWS, s), lambda bi, i: (bi, 0, 0, 0)),
            pl.BlockSpec((1, 1, s, kc.shape[3]), lambda bi, i: (bi, 0, 0, 0)), qspec,
            pl.BlockSpec((1, g, V_ROWS, s), lambda bi, i: (bi, 0, 0, 0)),
        ],
        out_specs=[ospec, ospec],
        out_shape=[jax.ShapeDtypeStruct((b, g, 64, s), BF16)] * 2,
        compiler_params=pltpu.CompilerParams(
            dimension_semantics=("arbitrary", "arbitrary"), vmem_limit_bytes=VMEM_LIMIT),
        name="banded",
    )(sink.astype(F32), bias, kb, qb, vb, kc, qc, vc)


def _out_kernel(oa_ref, ob_ref, oc_ref, od_ref, sg_ref, h_hbm, p_ref, lam_ref, sub_ref,
                wo_ref, pg_ref, wg_ref, wp_ref, fg_ref, y_ref, h_buf, h_sem, *, lam_init, final):
    nt = pl.num_programs(1)
    tm = h_buf.shape[1]
    step = pl.program_id(0) * nt + pl.program_id(1)
    total = pl.num_programs(0) * nt

    def h_copy(t):
        slot = t % H_SLOTS
        return pltpu.make_async_copy(h_hbm.at[t // nt, pl.ds((t % nt) * tm, tm), :], h_buf.at[slot], h_sem.at[slot])

    @pl.when(step == 0)
    def _():
        h_copy(step).start()

        @pl.when(total > 1)
        def _():
            h_copy(step + 1).start()

    @pl.when(step + 2 < total)
    def _():
        h_copy(step + 2).start()

    h_copy(step).wait()

    ple = _dot(p_ref[0, 0].astype(BF16), wp_ref[...])
    lv = lam_ref[...]
    lam = (jnp.exp(jnp.sum(lv[0:1] * lv[1:2], axis=-1, keepdims=True))
           - jnp.exp(jnp.sum(lv[2:3] * lv[3:4], axis=-1, keepdims=True)) + lam_init)
    parts = []
    for h in range(A_HEADS):
        o = oa_ref[0, 2 * h] - lam * oa_ref[0, 2 * h + 1]
        on = (o * lax.rsqrt(jnp.mean(o * o, axis=0, keepdims=True) + EPS)) * sub_ref[...]
        parts.append(on * (1.0 - lam_init))
    for ref in (ob_ref, oc_ref, od_ref):
        for h in range(4):
            parts.append(ref[0, h].astype(F32))
    mix = (jnp.concatenate(parts, axis=0) * sg_ref[0]).astype(BF16)
    h1 = h_buf[step % H_SLOTS] + _tn(mix, wo_ref[...])
    hn = ((h1 * lax.rsqrt(jnp.mean(h1 * h1, axis=-1, keepdims=True) + EPS)) * pg_ref[...]).astype(BF16)
    gate = jax.nn.sigmoid(_dot(hn, wg_ref[...]))
    h2 = h1 + gate * ple
    if final:
        h2 = (h2 * lax.rsqrt(jnp.mean(h2 * h2, axis=-1, keepdims=True) + EPS)) * fg_ref[...]
    y_ref[0] = h2


def _out(oa, ob, oc, od, sg, h, p, layer, a_lambda, subln_g, w_out, ple_norm_g, w_ple_gate, w_ple_proj,
         final_norm_g, lam_init, final):
    b, s, _ = h.shape
    tm = min(TM, s)

    def heads(n):
        return pl.BlockSpec((1, n, 64, tm), lambda bi, i: (bi, 0, 0, i))

    return pl.pallas_call(
        functools.partial(_out_kernel, lam_init=lam_init, final=final),
        grid=(b, s // tm),
        in_specs=[
            heads(8), heads(4), heads(4), heads(4),
            pl.BlockSpec((1, D_MIX, tm), lambda bi, i: (bi, 0, i)),
            pl.BlockSpec(memory_space=pl.ANY),
            pl.BlockSpec((1, 1, tm, D_PLE), lambda bi, i: (layer, bi, i, 0)),
            _const_spec((4, A_QK_DIM)), _const_spec((A_V_DIM, 1)),
            _const_spec((D_MIX, D_MODEL)), _const_spec((1, D_MODEL)),
            _const_spec((D_MODEL, D_MODEL)), _const_spec((D_PLE, D_MODEL)), _const_spec((1, D_MODEL)),
        ],
        out_specs=pl.BlockSpec((1, tm, D_MODEL), lambda bi, i: (bi, i, 0)),
        scratch_shapes=[pltpu.VMEM((H_SLOTS, tm, D_MODEL), F32), pltpu.SemaphoreType.DMA((H_SLOTS,))],
        out_shape=jax.ShapeDtypeStruct((b, s, D_MODEL), F32),
        compiler_params=pltpu.CompilerParams(
            dimension_semantics=("arbitrary", "arbitrary"), vmem_limit_bytes=VMEM_LIMIT),
        name="out",
    )(oa, ob, oc, od, sg, h, p, a_lambda.astype(F32), subln_g.reshape(A_V_DIM, 1).astype(F32),
      w_out.astype(BF16), ple_norm_g.reshape(1, D_MODEL), w_ple_gate.astype(BF16), w_ple_proj.astype(BF16),
      final_norm_g.reshape(1, D_MODEL))


def _trunk(x, p, tabs, preps, norm_g, a_lambda, a_subln_g, b_sink, c_rpb, d_q_norm_g, d_kv_norm_g,
           w_out, ple_norm_g, w_ple_gate, w_ple_proj, final_norm_g):
    depth = len(preps)
    s = x.shape[1]
    ta, tb, td, sa, sb, sd = tabs
    tabs = (ta[:, :, :s], tb[:, :, :s], td[:, :, :s], sa[:, :s], sb[:, :s], sd[:, :s])
    h = x
    for i in range(depth):
        (qa, ka, va, qb, kb, vb, qc, kc, vc, qd, kd, vd, sg) = _proj(
            h, norm_g[i], preps[i], tabs, d_q_norm_g[i], d_kv_norm_g[i])
        oa = _flash(ka, qa, va, lambda g: (0, 0, g // 4), lambda g: g // 2, F32)
        od = _flash(kd, qd, vd, lambda g: (0, 0, g), lambda g: g, BF16)
        ob, oc = _banded(kb, qb, vb, b_sink[i], kc, qc, vc, c_rpb[i])
        lam_init = 0.8 - 0.6 * math.exp(-0.3 * i)
        h = _out(oa, ob, oc, od, sg, h, p, i, a_lambda[i], a_subln_g[i], w_out[i], ple_norm_g[i],
                 w_ple_gate[i], w_ple_proj[i], final_norm_g, lam_init, i == depth - 1)
    return h


def kernel(x_prompt, x_sample, p_prompt, p_sample, norm_g, w_in, a_lambda, a_subln_g, b_sink, c_rpb,
           d_q_norm_g, d_kv_norm_g, d_w_uq, d_w_ukv, w_out, ple_norm_g, w_ple_gate, w_ple_proj,
           final_norm_g):
    tabs = _rope_tables(max(x_prompt.shape[1], x_sample.shape[1]))
    preps = [_prep_layer(w_in[i], d_w_uq[i], d_w_ukv[i]) for i in range(w_in.shape[0])]
    weights = (norm_g, a_lambda, a_subln_g, b_sink, c_rpb, d_q_norm_g, d_kv_norm_g,
               w_out, ple_norm_g, w_ple_gate, w_ple_proj, final_norm_g)
    return (_trunk(x_prompt, p_prompt, tabs, preps, *weights), _trunk(x_sample, p_sample, tabs, preps, *weights))
```

```python
import functools
import math

import jax
import jax.numpy as jnp
import numpy as np
from jax import lax
from jax.experimental import pallas as pl
from jax.experimental.pallas import tpu as pltpu

F32 = jnp.float32
BF16 = jnp.bfloat16

D_MODEL = 1024
D_MIX = 1024
D_PLE = 256
GRID_W = 64
ROPE_THETA = 10000.0
EPS = 1e-6
NEG_INF = -1e30

A_HEADS, A_QK_DIM, A_V_DIM = 4, 32, 64
B_HEADS, B_KV_HEADS, B_HEAD_DIM, B_WINDOW = 4, 2, 64, 128
C_HEADS, C_HEAD_DIM, C_MAX_ROWS, C_WIN_COLS = 4, 64, 8, 16
D_HEADS, D_Q_RANK, D_KV_RANK, D_NOPE, D_ROPE, D_V = 4, 256, 128, 64, 32, 64

IN_SIZES = (256, 256, 256, 256, 128, 128, 256, 256, 256, D_Q_RANK, D_KV_RANK, D_ROPE, D_MIX)

V_ROWS = 80
VMEM_LIMIT = 56 * 1024 * 1024

TM = 512
TQ_DENSE = 512
LOG2E = math.log2(math.e)
D_SCALE = (D_NOPE + D_ROPE) ** -0.5 * LOG2E
TK = 256
FLASH_QBLOCKS = 8
FLASH_BUF_BYTES = 24 * 1024 * 1024
C_QROWS = 4
C_KROWS = 12


def _nt(a, b):
    return lax.dot_general(a, b, (((1,), (1,)), ((), ())), preferred_element_type=F32)


def _tn(a, b):
    return lax.dot_general(a, b, (((0,), (0,)), ((), ())), preferred_element_type=F32)


def _dot(a, b):
    return jnp.dot(a, b, preferred_element_type=F32)


T_AQ, T_BQ, T_CQ, T_DCQ, T_DCKV, T_AV, T_BV, T_CV, T_GATE, T_END = (
    0, 256, 512, 768, 1024, 1152, 1408, 1536, 1792, 2816)
S_AK, S_BK, S_CK, S_DCKV, S_KPE, S_END = (0, 256, 384, 640, 768, 896)


def _swap_row_halves(x, d):
    h = d // 2
    return jnp.concatenate([x[r + o:r + o + h] for r in range(0, x.shape[0], d) for o in (h, 0)], axis=0)


def _swap_lane_halves(x, d):
    n = x.shape[1]
    lane = lax.broadcasted_iota(jnp.int32, x.shape, 1)
    return jnp.where(lane % d < d // 2, pltpu.roll(x, n - d // 2, 1), pltpu.roll(x, d // 2, 1))


def _proj_kernel(h_ref, g_ref, wt_ref, ws_ref, ta_ref, tb_ref, td_ref, sa_ref, sb_ref, sd_ref,
                 qg_ref, kvgc_ref, kvgr_ref, wqn_ref, wqp_ref, wkn_ref, place_ref, wvt_ref,
                 qa_ref, ka_ref, va_ref, qb_ref, kb_ref, vb_ref, qc_ref, kc_ref, vc_ref,
                 qd_ref, kd_ref, vd_ref, sg_ref):
    x = h_ref[0]
    tm = x.shape[0]
    ms = jnp.mean(x * x, axis=-1, keepdims=True)
    hn = ((x * lax.rsqrt(ms + EPS)) * g_ref[...]).astype(BF16)

    zs = _dot(hn, ws_ref[...])
    zak, zbk, zpe = zs[:, S_AK:S_BK], zs[:, S_BK:S_CK], zs[:, S_KPE:S_END]
    ka_ref[0, 0] = (zak * sa_ref[0] + _swap_lane_halves(zak, A_QK_DIM) * sa_ref[1]).astype(BF16)
    kb_ref[0, 0] = (zbk * sb_ref[0] + _swap_lane_halves(zbk, B_HEAD_DIM) * sb_ref[1]).astype(BF16)
    kc_ref[0, 0] = zs[:, S_CK:S_DCKV].astype(BF16)
    ck = zs[:, S_DCKV:S_KPE]
    ckn = ((ck * lax.rsqrt(jnp.mean(ck * ck, axis=-1, keepdims=True) + EPS)) * kvgr_ref[...]).astype(BF16)
    kpe = (zpe * sd_ref[0] + _swap_lane_halves(zpe, D_ROPE) * sd_ref[1]).astype(BF16)
    kd_ref[0, 0] = (_dot(ckn, wkn_ref[...]) + _dot(kpe, place_ref[...])).astype(BF16)

    zq = _nt(wt_ref[T_AQ:T_AV, :], hn)
    zv = _nt(wt_ref[T_AV:T_GATE, :], hn)
    gt = _nt(wt_ref[T_GATE:T_END, :], hn)
    sg_ref[0] = (gt * jax.nn.sigmoid(gt)).astype(BF16)

    ones_rows = (lax.broadcasted_iota(jnp.int32, (V_ROWS - 64, tm), 0) == 0).astype(BF16)

    def store_values(ref, v, heads):
        for h in range(heads):
            ref[0, h, 0:64, :] = v[64 * h:64 * h + 64].astype(BF16)
            ref[0, h, 64:V_ROWS, :] = ones_rows

    za = zq[T_AQ:T_BQ]
    qa = (za * jnp.tile(ta_ref[0], (2 * A_HEADS, 1))
          + _swap_row_halves(za, A_QK_DIM) * jnp.tile(ta_ref[1], (2 * A_HEADS, 1)))
    for g in range(2 * A_HEADS):
        r = 32 * (g % 4)
        qa_ref[0, g] = jnp.zeros((128, tm), BF16)
        qa_ref[0, g, r:r + 32, :] = qa[32 * g:32 * g + 32].astype(BF16)
    store_values(va_ref, zv[0:T_BV - T_AV], A_HEADS)

    zb = zq[T_BQ:T_CQ]
    qb = (zb * jnp.tile(tb_ref[0], (B_HEADS, 1))
          + _swap_row_halves(zb, B_HEAD_DIM) * jnp.tile(tb_ref[1], (B_HEADS, 1)))
    for h in range(B_HEADS):
        r = 64 * (h // 2)
        qb_ref[0, h] = jnp.zeros((128, tm), BF16)
        qb_ref[0, h, r:r + 64, :] = qb[64 * h:64 * h + 64].astype(BF16)
    store_values(vb_ref, zv[T_BV - T_AV:T_CV - T_AV], B_KV_HEADS)

    qc = zq[T_CQ:T_DCQ] * (C_HEAD_DIM ** -0.5 * LOG2E)
    for h in range(C_HEADS):
        r = 64 * (h % 2)
        qc_ref[0, h] = jnp.zeros((128, tm), BF16)
        qc_ref[0, h, r:r + 64, :] = qc[64 * h:64 * h + 64].astype(BF16)
    store_values(vc_ref, zv[T_CV - T_AV:T_GATE - T_AV], C_HEADS)

    cq = zq[T_DCQ:T_DCKV]
    cqn = ((cq * lax.rsqrt(jnp.mean(cq * cq, axis=0, keepdims=True) + EPS)) * qg_ref[...]).astype(BF16)
    for h in range(D_HEADS):
        qn = _dot(wqn_ref[h], cqn) * D_SCALE
        qp = _dot(wqp_ref[h], cqn)
        qpe = qp * td_ref[0] + _swap_row_halves(qp, D_ROPE) * td_ref[1]
        qd_ref[0, h, 0:64, :] = qn.astype(BF16)
        qd_ref[0, h, 64:96, :] = qpe.astype(BF16)
        qd_ref[0, h, 96:128, :] = jnp.zeros((32, tm), BF16)
    ckt = zq[T_DCKV:T_AV]
    cktn = ((ckt * lax.rsqrt(jnp.mean(ckt * ckt, axis=0, keepdims=True) + EPS)) * kvgc_ref[...]).astype(BF16)
    store_values(vd_ref, _dot(wvt_ref[...], cktn), D_HEADS)


def _rope_cs(s, dim):
    inv = 1.0 / (ROPE_THETA ** (jnp.arange(0, dim, 2, dtype=F32) / dim))
    ang = jnp.arange(s, dtype=F32)[:, None] * inv[None, :]
    cos, sin = jnp.cos(ang), jnp.sin(ang)
    return jnp.concatenate([cos, cos], -1), jnp.concatenate([-sin, sin], -1)


def _rope_tables(s):
    c32, s32 = _rope_cs(s, A_QK_DIM)
    c64, s64 = _rope_cs(s, B_HEAD_DIM)
    a_scale, b_scale, d_scale = A_QK_DIM ** -0.5 * LOG2E, B_HEAD_DIM ** -0.5 * LOG2E, D_SCALE
    ta = jnp.stack([c32.T, s32.T]) * a_scale
    tb = jnp.stack([c64.T, s64.T]) * b_scale
    td = jnp.stack([c32.T, s32.T]) * d_scale
    sa = jnp.stack([jnp.tile(c32, (1, 8)), jnp.tile(s32, (1, 8))])
    sb = jnp.stack([jnp.tile(c64, (1, 2)), jnp.tile(s64, (1, 2))])
    pad = jnp.zeros((s, 128 - D_ROPE), F32)
    sd = jnp.stack([jnp.concatenate([c32, pad], -1), jnp.concatenate([s32, pad], -1)])
    return ta, tb, td, sa, sb, sd


def _prep_layer(w_in, d_w_uq, d_w_ukv):
    pts = [0]
    for n in IN_SIZES:
        pts.append(pts[-1] + n)
    (aq, ak, av, bq, bk, bv, cq, ck, cv, dcq, dckv, dkr, gate) = [w_in[:, pts[i]:pts[i + 1]] for i in range(13)]
    wt = jnp.concatenate([aq, bq, cq, dcq, dckv, av, bv, cv, gate], axis=1).T.astype(BF16)
    zpad = jnp.zeros((D_MODEL, 128 - D_ROPE), F32)
    ws = jnp.concatenate([ak, bk, ck, dckv, dkr, zpad], axis=1).astype(BF16)
    uq = d_w_uq.reshape(D_Q_RANK, D_HEADS, D_NOPE + D_ROPE)
    uq_n, uq_p = uq[:, :, :D_NOPE], uq[:, :, D_NOPE:]
    wqn = jnp.transpose(uq_n, (1, 2, 0))
    wqp = jnp.transpose(uq_p, (1, 2, 0))
    ukv = d_w_ukv.reshape(D_KV_RANK, D_HEADS, D_NOPE + D_V)
    wkn = jnp.pad(ukv[:, :, :D_NOPE], ((0, 0), (0, 0), (0, 128 - D_NOPE))).reshape(D_KV_RANK, D_HEADS * 128)
    place = np.zeros((128, D_HEADS, 128), np.float32)
    place[np.arange(D_ROPE), :, D_NOPE + np.arange(D_ROPE)] = 1.0
    place = jnp.asarray(place.reshape(128, D_HEADS * 128), BF16)
    wvt = ukv[:, :, D_NOPE:].reshape(D_KV_RANK, D_HEADS * D_V).T
    return (wt, ws, wqn.astype(BF16), wqp.astype(BF16), wkn.astype(BF16), place, wvt.astype(BF16))


def _const_spec(shape):
    nd = len(shape)
    return pl.BlockSpec(shape, lambda *_: (0,) * nd, pipeline_mode=pl.Buffered(1))


def _proj(h, norm_g, prep, tabs, q_norm_g, kv_norm_g):
    b, s, _ = h.shape
    tm = min(TM, s)
    wt, ws, wqn, wqp, wkn, place, wvt = prep
    ta, tb, td, sa, sb, sd = tabs

    def tok(shape_tail):
        return pl.BlockSpec((1, 1, tm) + shape_tail, lambda i, j: (j, 0, i, 0))

    def chan(groups, rows):
        return pl.BlockSpec((1, groups, rows, tm), lambda i, j: (j, 0, 0, i))

    in_specs = [
        pl.BlockSpec((1, tm, D_MODEL), lambda i, j: (j, i, 0)),
        _const_spec((1, D_MODEL)), _const_spec(wt.shape), _const_spec(ws.shape),
        pl.BlockSpec((2, A_QK_DIM, tm), lambda i, j: (0, 0, i)),
        pl.BlockSpec((2, B_HEAD_DIM, tm), lambda i, j: (0, 0, i)),
        pl.BlockSpec((2, D_ROPE, tm), lambda i, j: (0, 0, i)),
        pl.BlockSpec((2, tm, 256), lambda i, j: (0, i, 0)),
        pl.BlockSpec((2, tm, 128), lambda i, j: (0, i, 0)),
        pl.BlockSpec((2, tm, 128), lambda i, j: (0, i, 0)),
        _const_spec((D_Q_RANK, 1)), _const_spec((D_KV_RANK, 1)), _const_spec((1, D_KV_RANK)),
        _const_spec(wqn.shape), _const_spec(wqp.shape),
        _const_spec(wkn.shape), _const_spec(place.shape), _const_spec(wvt.shape),
    ]
    out_shape = [
        jax.ShapeDtypeStruct((b, 8, 128, s), BF16),
        jax.ShapeDtypeStruct((b, 1, s, 256), BF16),
        jax.ShapeDtypeStruct((b, A_HEADS, V_ROWS, s), BF16),
        jax.ShapeDtypeStruct((b, B_HEADS, 128, s), BF16),
        jax.ShapeDtypeStruct((b, 1, s, 128), BF16),
        jax.ShapeDtypeStruct((b, B_KV_HEADS, V_ROWS, s), BF16),
        jax.ShapeDtypeStruct((b, C_HEADS, 128, s), BF16),
        jax.ShapeDtypeStruct((b, 1, s, 256), BF16),
        jax.ShapeDtypeStruct((b, C_HEADS, V_ROWS, s), BF16),
        jax.ShapeDtypeStruct((b, D_HEADS, 128, s), BF16),
        jax.ShapeDtypeStruct((b, 1, s, D_HEADS * 128), BF16),
        jax.ShapeDtypeStruct((b, D_HEADS, V_ROWS, s), BF16),
        jax.ShapeDtypeStruct((b, D_MIX, s), BF16),
    ]
    out_specs = [
        chan(8, 128), tok((256,)), chan(A_HEADS, V_ROWS),
        chan(B_HEADS, 128), tok((128,)), chan(B_KV_HEADS, V_ROWS),
        chan(C_HEADS, 128), tok((256,)), chan(C_HEADS, V_ROWS),
        chan(D_HEADS, 128), tok((D_HEADS * 128,)), chan(D_HEADS, V_ROWS),
        pl.BlockSpec((1, D_MIX, tm), lambda i, j: (j, 0, i)),
    ]
    return pl.pallas_call(
        _proj_kernel,
        grid=(s // tm, b),
        in_specs=in_specs, out_specs=out_specs, out_shape=out_shape,
        compiler_params=pltpu.CompilerParams(
            dimension_semantics=("arbitrary", "arbitrary"), vmem_limit_bytes=VMEM_LIMIT),
        name="proj",
    )(h, norm_g.reshape(1, D_MODEL), wt, ws, ta, tb, td, sa, sb, sd,
      q_norm_g.reshape(D_Q_RANK, 1), kv_norm_g.reshape(D_KV_RANK, 1), kv_norm_g.reshape(1, D_KV_RANK),
      wqn, wqp, wkn, place, wvt)


def _flash_kernel(k_ref, q_ref, v_ref, o_ref, s_buf, p_buf, pv_buf, *, tk, nk, tq):
    total = (q_ref.shape[3] // tq) * nk

    def scores(t):
        b, j = divmod(t, nk)
        s = _dot(k_ref[0, 0, j * tk:(j + 1) * tk, :], q_ref[0, 0, :, b * tq:(b + 1) * tq])
        s_buf[j] = s
        return jnp.max(s, axis=0, keepdims=True)

    def softmax(t, m, cmax):
        j = t % nk
        m_new = jnp.maximum(m, cmax)
        p_buf[j] = jnp.exp2(s_buf[j] - m_new).astype(BF16)
        return m_new, jnp.exp2(m - m_new)

    def values(t, acc, alpha):
        j = t % nk
        pv_buf[...] = _dot(v_ref[0, 0, :, j * tk:(j + 1) * tk], p_buf[j])
        return acc * alpha + pv_buf[...]

    m_init = jnp.full((1, tq), NEG_INF, F32)
    acc_init = jnp.zeros((V_ROWS, tq), F32)
    m, acc = m_init, acc_init
    cmax = {0: scores(0)}
    alpha = {}
    for t in range(total + 1):
        if t + 1 < total:
            cmax[t + 1] = scores(t + 1)
        if t < total:
            if t % nk == 0:
                m = m_init
            m, alpha[t] = softmax(t, m, cmax.pop(t))
        if t >= 1:
            if (t - 1) % nk == 0:
                acc = acc_init
            acc = values(t - 1, acc, alpha.pop(t - 1))
            if t % nk == 0:
                b = t // nk - 1
                o_ref[0, 0, :, b * tq:(b + 1) * tq] = (acc[0:64] / acc[64:65]).astype(o_ref.dtype)


def _flash(k, qt, vt, k_group, v_group, out_dtype):
    b, g, dk, s = qt.shape
    tq, tk = min(TQ_DENSE, s), min(TK, s)
    nk = s // tk
    slots = nk
    assert slots * tk * tq * 6 <= FLASH_BUF_BYTES
    tqs = tq * math.gcd(s // tq, FLASH_QBLOCKS)
    return pl.pallas_call(
        functools.partial(_flash_kernel, tk=tk, nk=nk, tq=tq),
        grid=(b, g, s // tqs),
        scratch_shapes=[pltpu.VMEM((slots, tk, tq), F32), pltpu.VMEM((slots, tk, tq), BF16),
                        pltpu.VMEM((V_ROWS, tq), F32)],
        in_specs=[
            pl.BlockSpec((1, 1, s, dk), lambda bi, gi, i: (bi,) + k_group(gi)),
            pl.BlockSpec((1, 1, dk, tqs), lambda bi, gi, i: (bi, gi, 0, i)),
            pl.BlockSpec((1, 1, V_ROWS, s), lambda bi, gi, i: (bi, v_group(gi), 0, 0)),
        ],
        out_specs=pl.BlockSpec((1, 1, 64, tqs), lambda bi, gi, i: (bi, gi, 0, i)),
        out_shape=jax.ShapeDtypeStruct((b, g, 64, s), out_dtype),
        compiler_params=pltpu.CompilerParams(
            dimension_semantics=("arbitrary", "arbitrary", "arbitrary"), vmem_limit_bytes=VMEM_LIMIT),
        name="flash",
    )(k, qt, vt)


def _banded_kernel(sink_ref, bias_ref, kb_ref, qb_ref, vb_ref, kc_ref, qc_ref, vc_ref, ob_ref, oc_ref, s_buf,
                   *, s_len, win_b, win_c):
    i = pl.program_id(1)
    tq = qb_ref.shape[3]
    q0 = i * tq
    start_b = pl.multiple_of(jnp.clip(q0 - B_WINDOW, 0, s_len - win_b), 128)
    start_c = pl.multiple_of(jnp.clip(q0 - (C_MAX_ROWS // 2) * GRID_W, 0, s_len - win_c), 256)
    kb = kb_ref[0, 0, pl.ds(start_b, win_b), :]
    kpos = start_b + lax.broadcasted_iota(jnp.int32, (win_b, tq), 0)
    qpos = q0 + lax.broadcasted_iota(jnp.int32, (win_b, tq), 1)
    in_band = jnp.abs(kpos - qpos) <= B_WINDOW
    sc_b = [_dot(kb, qb_ref[0, h]) for h in range(B_HEADS)]
    for h in range(C_HEADS):
        k = kc_ref[0, 0, pl.ds(start_c, win_c), 128 * (h // 2):128 * (h // 2) + 128]
        s_buf[h] = _dot(k, qc_ref[0, h])
    for h in range(B_HEADS):
        s = jnp.where(in_band, sc_b[h], NEG_INF)
        sink = sink_ref[h] * LOG2E
        m = jnp.maximum(jnp.max(s, axis=0, keepdims=True), sink)
        p = jnp.exp2(s - m).astype(BF16)
        acc = _dot(vb_ref[0, h // (B_HEADS // B_KV_HEADS), :, pl.ds(start_b, win_b)], p)
        ob_ref[0, h] = (acc[0:64] / (acc[64:65] + jnp.exp2(sink - m))).astype(ob_ref.dtype)
        s = s_buf[h] + bias_ref[0, h]
        m = jnp.max(s, axis=0, keepdims=True)
        p = jnp.exp2(s - m).astype(BF16)
        acc = _dot(vc_ref[0, h, :, pl.ds(start_c, win_c)], p)
        oc_ref[0, h] = (acc[0:64] / acc[64:65]).astype(oc_ref.dtype)


def _nbr_bias(rpb, rows):
    col = np.arange(GRID_W)
    cs = np.clip(col - C_WIN_COLS // 2, 0, GRID_W - C_WIN_COLS)
    col_ok = (col[:, None] >= cs[None, :]) & (col[:, None] < cs[None, :] + C_WIN_COLS)
    dc = np.clip(col[:, None] - col[None, :], -(C_WIN_COLS - 1), C_WIN_COLS - 1) + (C_WIN_COLS - 1)
    rp = rpb.astype(F32)
    by_cols = jnp.zeros(rp.shape[:2] + dc.shape, F32)
    for b in range(rp.shape[2]):
        by_cols = by_cols + jnp.where((dc == b)[None, None], rp[:, :, b][:, :, None, None], 0.0)
    kr = min(C_MAX_ROWS, rows)
    out = []
    for r0 in (0, C_QROWS, rows - C_QROWS):
        rk0 = min(max(r0 - C_MAX_ROWS // 2, 0), rows - C_KROWS)
        rq = r0 + np.arange(C_QROWS)
        rk = rk0 + np.arange(C_KROWS)
        rs = np.clip(rq - kr // 2, 0, rows - kr)
        row_ok = (rk[:, None] >= rs[None, :]) & (rk[:, None] < rs[None, :] + kr)
        dr = np.clip(rk[:, None] - rq[None, :] + (C_MAX_ROWS - 1), 0, 2 * C_MAX_ROWS - 2)
        blocks = jnp.stack([jnp.stack([by_cols[:, dr[a, c]] for c in range(C_QROWS)], axis=2)
                            for a in range(C_KROWS)], axis=1)
        ok = row_ok[:, None, :, None] & col_ok[None, :, None, :]
        bias = jnp.where(ok[None], blocks * LOG2E, NEG_INF)
        out.append(bias.reshape(rp.shape[0], C_KROWS * GRID_W, C_QROWS * GRID_W))
    return jnp.stack(out)


def _banded(kb, qb, vb, sink, kc, qc, vc, rpb):
    b, g, dk, s = qb.shape
    tq = C_QROWS * GRID_W
    win_b = tq + 2 * B_WINDOW
    win_c = C_KROWS * GRID_W
    nq = s // tq
    bias = _nbr_bias(rpb, s // GRID_W)

    def variant(i):
        return jnp.where(i == 0, 0, jnp.where(i == nq - 1, 2, 1))

    qspec = pl.BlockSpec((1, g, dk, tq), lambda bi, i: (bi, 0, 0, i))
    ospec = pl.BlockSpec((1, g, 64, tq), lambda bi, i: (bi, 0, 0, i))
    return pl.pallas_call(
        functools.partial(_banded_kernel, s_len=s, win_b=win_b, win_c=win_c),
        grid=(b, nq),
        scratch_shapes=[pltpu.VMEM((g, win_c, tq), F32)],
        in_specs=[
            pl.BlockSpec(memory_space=pltpu.SMEM),
            pl.BlockSpec((1, g, win_c, tq), lambda bi, i: (variant(i), 0, 0, 0)),
            pl.BlockSpec((1, 1, s, kb.shape[3]), lambda bi, i: (bi, 0, 0, 0)), qspec,
            pl.BlockSpec((1, B_KV_HEADS, V_ROWS, s), lambda bi, i: (bi, 0, 0, 0)),
            pl.BlockSpec((1, 1, s, kc.shape[3]), lambda bi, i: (bi, 0, 0, 0)), qspec,
            pl.BlockSpec((1, g, V_ROWS, s), lambda bi, i: (bi, 0, 0, 0)),
        ],
        out_specs=[ospec, ospec],
        out_shape=[jax.ShapeDtypeStruct((b, g, 64, s), BF16)] * 2,
        compiler_params=pltpu.CompilerParams(
            dimension_semantics=("arbitrary", "arbitrary"), vmem_limit_bytes=VMEM_LIMIT),
        name="banded",
    )(sink.astype(F32), bias, kb, qb, vb, kc, qc, vc)


def _out_kernel(oa_ref, ob_ref, oc_ref, od_ref, sg_ref, h_ref, p_ref, lam_ref, sub_ref,
                wo_ref, pg_ref, wg_ref, wp_ref, fg_ref, y_ref, *, lam_init, final):
    ple = _dot(p_ref[0, 0].astype(BF16), wp_ref[...])
    lv = lam_ref[...]
    lam = (jnp.exp(jnp.sum(lv[0:1] * lv[1:2], axis=-1, keepdims=True))
           - jnp.exp(jnp.sum(lv[2:3] * lv[3:4], axis=-1, keepdims=True)) + lam_init)
    parts = []
    for h in range(A_HEADS):
        o = oa_ref[0, 2 * h] - lam * oa_ref[0, 2 * h + 1]
        on = (o * lax.rsqrt(jnp.mean(o * o, axis=0, keepdims=True) + EPS)) * sub_ref[...]
        parts.append(on * (1.0 - lam_init))
    for ref in (ob_ref, oc_ref, od_ref):
        for h in range(4):
            parts.append(ref[0, h].astype(F32))
    mix = (jnp.concatenate(parts, axis=0) * sg_ref[0]).astype(BF16)
    h1 = h_ref[0] + _tn(mix, wo_ref[...])
    hn = ((h1 * lax.rsqrt(jnp.mean(h1 * h1, axis=-1, keepdims=True) + EPS)) * pg_ref[...]).astype(BF16)
    gate = jax.nn.sigmoid(_dot(hn, wg_ref[...]))
    h2 = h1 + gate * ple
    if final:
        h2 = (h2 * lax.rsqrt(jnp.mean(h2 * h2, axis=-1, keepdims=True) + EPS)) * fg_ref[...]
    y_ref[0] = h2


def _out(oa, ob, oc, od, sg, h, p, layer, a_lambda, subln_g, w_out, ple_norm_g, w_ple_gate, w_ple_proj,
         final_norm_g, lam_init, final):
    b, s, _ = h.shape
    tm = min(TM, s)

    def heads(n):
        return pl.BlockSpec((1, n, 64, tm), lambda bi, i: (bi, 0, 0, i))

    return pl.pallas_call(
        functools.partial(_out_kernel, lam_init=lam_init, final=final),
        grid=(b, s // tm),
        in_specs=[
            heads(8), heads(4), heads(4), heads(4),
            pl.BlockSpec((1, D_MIX, tm), lambda bi, i: (bi, 0, i)),
            pl.BlockSpec((1, tm, D_MODEL), lambda bi, i: (bi, i, 0)),
            pl.BlockSpec((1, 1, tm, D_PLE), lambda bi, i: (layer, bi, i, 0)),
            _const_spec((4, A_QK_DIM)), _const_spec((A_V_DIM, 1)),
            _const_spec((D_MIX, D_MODEL)), _const_spec((1, D_MODEL)),
            _const_spec((D_MODEL, D_MODEL)), _const_spec((D_PLE, D_MODEL)), _const_spec((1, D_MODEL)),
        ],
        out_specs=pl.BlockSpec((1, tm, D_MODEL), lambda bi, i: (bi, i, 0)),
        out_shape=jax.ShapeDtypeStruct((b, s, D_MODEL), F32),
        compiler_params=pltpu.CompilerParams(
            dimension_semantics=("arbitrary", "arbitrary"), vmem_limit_bytes=VMEM_LIMIT),
        name="out",
    )(oa, ob, oc, od, sg, h, p, a_lambda.astype(F32), subln_g.reshape(A_V_DIM, 1).astype(F32),
      w_out.astype(BF16), ple_norm_g.reshape(1, D_MODEL), w_ple_gate.astype(BF16), w_ple_proj.astype(BF16),
      final_norm_g.reshape(1, D_MODEL))


def _trunk(x, p, tabs, preps, norm_g, a_lambda, a_subln_g, b_sink, c_rpb, d_q_norm_g, d_kv_norm_g,
           w_out, ple_norm_g, w_ple_gate, w_ple_proj, final_norm_g):
    depth = len(preps)
    s = x.shape[1]
    ta, tb, td, sa, sb, sd = tabs
    tabs = (ta[:, :, :s], tb[:, :, :s], td[:, :, :s], sa[:, :s], sb[:, :s], sd[:, :s])
    h = x
    for i in range(depth):
        (qa, ka, va, qb, kb, vb, qc, kc, vc, qd, kd, vd, sg) = _proj(
            h, norm_g[i], preps[i], tabs, d_q_norm_g[i], d_kv_norm_g[i])
        oa = _flash(ka, qa, va, lambda g: (0, 0, g // 4), lambda g: g // 2, F32)
        od = _flash(kd, qd, vd, lambda g: (0, 0, g), lambda g: g, BF16)
        ob, oc = _banded(kb, qb, vb, b_sink[i], kc, qc, vc, c_rpb[i])
        lam_init = 0.8 - 0.6 * math.exp(-0.3 * i)
        h = _out(oa, ob, oc, od, sg, h, p, i, a_lambda[i], a_subln_g[i], w_out[i], ple_norm_g[i],
                 w_ple_gate[i], w_ple_proj[i], final_norm_g, lam_init, i == depth - 1)
    return h


def kernel(x_prompt, x_sample, p_prompt, p_sample, norm_g, w_in, a_lambda, a_subln_g, b_sink, c_rpb,
           d_q_norm_g, d_kv_norm_g, d_w_uq, d_w_ukv, w_out, ple_norm_g, w_ple_gate, w_ple_proj,
           final_norm_g):
    tabs = _rope_tables(max(x_prompt.shape[1], x_sample.shape[1]))
    preps = [_prep_layer(w_in[i], d_w_uq[i], d_w_ukv[i]) for i in range(w_in.shape[0])]
    weights = (norm_g, a_lambda, a_subln_g, b_sink, c_rpb, d_q_norm_g, d_kv_norm_g,
               w_out, ple_norm_g, w_ple_gate, w_ple_proj, final_norm_g)
    return (_trunk(x_prompt, p_prompt, tabs, preps, *weights), _trunk(x_sample, p_sample, tabs, preps, *weights))
```
